```python
import jax
import jax.numpy as jnp
from jax import lax
import numpy as np

D_MODEL = 4096
BATCH = 4
SEQ = 2048
DEPTH = 2

GRID_W = 64
CTX_LEN = 256
MIX_W = D_MODEL
N_MIXERS = 4
GROUP_W = MIX_W // N_MIXERS
HEAD_DIM = 128
N_GROUP_HEADS = GROUP_W // HEAD_DIM
GQA_KV_HEADS = 2
GQA_KV_W = GQA_KV_HEADS * HEAD_DIM
CONV_WIDTH = 31
CHUNK = 128
Q_BLOCK = 128
NA_WIN_R = 8
NA_WIN_C = 16
ROPE_THETA = 10000.0
NORM_EPS = 1e-6
LN_EPS = 1e-5
ATTN_SCALE = HEAD_DIM ** -0.5
MASK_VALUE = -1e30
RET_BASE_EXP = 5.0

COL_SIZES = (
    2 * GROUP_W, GROUP_W,
    GROUP_W, GQA_KV_W, GQA_KV_W, GROUP_W,
    GROUP_W, GROUP_W, GROUP_W, GROUP_W,
    GROUP_W, GROUP_W, GROUP_W, GROUP_W,
)
P_TOTAL = 13 * GROUP_W + 2 * GQA_KV_W

kernel_name = 'hybrid_conv_gqa_retention_natten_dit'


def _rmsnorm(x, g):
    xf = x.astype(jnp.float32)
    y = xf * lax.rsqrt(jnp.mean(xf * xf, axis=-1, keepdims=True) + NORM_EPS)
    return (y * g.astype(jnp.float32)).astype(x.dtype)


def _layernorm(x, g, b):
    xf = x.astype(jnp.float32)
    mu = jnp.mean(xf, axis=-1, keepdims=True)
    var = jnp.mean(jnp.square(xf - mu), axis=-1, keepdims=True)
    y = (xf - mu) * lax.rsqrt(var + LN_EPS)
    return (y * g.astype(jnp.float32) + b.astype(jnp.float32)).astype(x.dtype)


def _split_cols(p):
    out = []
    start = 0
    for n in COL_SIZES:
        out.append(p[..., start:start + n])
        start += n
    return out


def _to_heads(a, n_heads):
    bsz, n_tok, _ = a.shape
    return a.reshape(bsz, n_tok, n_heads, HEAD_DIM).transpose(0, 2, 1, 3)


def _from_heads(a):
    bsz, n_heads, n_tok, dh = a.shape
    return a.transpose(0, 2, 1, 3).reshape(bsz, n_tok, n_heads * dh)


def _axial_rope_tables(n_tok):
    t = jnp.arange(n_tok)
    row = (t // GRID_W).astype(jnp.float32)
    col = (t % GRID_W).astype(jnp.float32)
    half = HEAD_DIM // 2
    inv_freq = ROPE_THETA ** (-jnp.arange(0, half, 2, dtype=jnp.float32) / half)
    ang_r = row[:, None] * inv_freq[None, :]
    ang_c = col[:, None] * inv_freq[None, :]
    ang = jnp.concatenate([ang_r, ang_r, ang_c, ang_c], axis=-1)
    return jnp.cos(ang), jnp.sin(ang)


def _apply_rope(x, cos, sin):
    x1, x2, x3, x4 = jnp.split(x, 4, axis=-1)
    rot = jnp.concatenate([-x2, x1, -x4, x3], axis=-1)
    return x * cos.astype(x.dtype) + rot * sin.astype(x.dtype)


def _softmax_attend(q, k, v):
    s = jnp.einsum('bkgqd,bksd->bkgqs', q, k).astype(jnp.float32) * ATTN_SCALE
    p = jax.nn.softmax(s, axis=-1).astype(v.dtype)
    return jnp.einsum('bkgqs,bksd->bkgqd', p, v)


def _conv_mixer(glu, gate, w, b, ln_g, ln_b):
    a, g = jnp.split(glu, 2, axis=-1)
    u = a * jax.nn.sigmoid(g)
    u = lax.conv_general_dilated(
        u, w[:, None, :], window_strides=(1,),
        padding=[(CONV_WIDTH // 2, CONV_WIDTH // 2)],
        dimension_numbers=('NWC', 'WIO', 'NWC'),
        feature_group_count=GROUP_W) + b
    u = _layernorm(u, ln_g, ln_b)
    return jax.nn.silu(u) * jax.nn.silu(gate)


def _gqa_mixer(lat, ctx, qn_g, kn_g, cos, sin, with_ctx):
    ql, kl, vl, gl = lat
    qc, kc, vc, gc = ctx
    bsz, n_tok, _ = ql.shape
    n_grp = N_GROUP_HEADS // GQA_KV_HEADS
    ql_h = _apply_rope(_rmsnorm(_to_heads(ql, N_GROUP_HEADS), qn_g), cos, sin)
    kl_h = _apply_rope(_rmsnorm(_to_heads(kl, GQA_KV_HEADS), kn_g), cos, sin)
    kc_h = _rmsnorm(_to_heads(kc, GQA_KV_HEADS), kn_g)
    vl_h = _to_heads(vl, GQA_KV_HEADS)
    vc_h = _to_heads(vc, GQA_KV_HEADS)
    k_all = jnp.concatenate([kc_h, kl_h], axis=2)
    v_all = jnp.concatenate([vc_h, vl_h], axis=2)
    n_blk = n_tok // Q_BLOCK
    qb = ql_h.reshape(bsz, GQA_KV_HEADS, n_grp, n_blk, Q_BLOCK, HEAD_DIM).transpose(3, 0, 1, 2, 4, 5)
    ob = lax.map(lambda q_blk: _softmax_attend(q_blk, k_all, v_all), qb)
    ol = ob.transpose(1, 2, 3, 0, 4, 5).reshape(bsz, N_GROUP_HEADS, n_tok, HEAD_DIM)
    y_lat = _from_heads(ol) * jax.nn.silu(gl)
    y_ctx = None
    if with_ctx:
        n_ctx = qc.shape[1]
        qc_h = _rmsnorm(_to_heads(qc, N_GROUP_HEADS), qn_g)
        oc = _softmax_attend(qc_h.reshape(bsz, GQA_KV_HEADS, n_grp, n_ctx, HEAD_DIM), kc_h, vc_h)
        y_ctx = _from_heads(oc.reshape(bsz, N_GROUP_HEADS, n_ctx, HEAD_DIM)) * jax.nn.silu(gc)
    return y_lat, y_ctx


def _chunk_retention(q, k, v, log_g, s0):
    bsz, n_heads, n_tok, dh = q.shape
    n_chunk = n_tok // CHUNK
    idx = jnp.arange(CHUNK, dtype=jnp.float32)
    diff = idx[:, None] - idx[None, :]
    decay_in = jnp.where(diff[None] >= 0,
                         jnp.exp(jnp.maximum(diff, 0.0)[None] * log_g[:, None, None]), 0.0)
    q_dec = jnp.exp((idx + 1.0)[None, :] * log_g[:, None])[None, :, :, None]
    k_dec = jnp.exp((CHUNK - 1.0 - idx)[None, :] * log_g[:, None])[None, :, :, None]
    c_dec = jnp.exp(CHUNK * log_g)[None, :, None, None]

    def chunks(a):
        return a.reshape(bsz, n_heads, n_chunk, CHUNK, dh).transpose(2, 0, 1, 3, 4)

    def step(s, qkv):
        qc, kc, vc = qkv
        att = jnp.einsum('bhid,bhjd->bhij', qc, kc) * decay_in
        o = jnp.einsum('bhij,bhjd->bhid', att, vc) + jnp.einsum('bhid,bhde->bhie', qc * q_dec, s)
        s = s * c_dec + jnp.einsum('bhjd,bhje->bhde', kc * k_dec, vc)
        return s, o

    s_fin, o = lax.scan(step, s0, (chunks(q), chunks(k), chunks(v)))
    o = o.transpose(1, 2, 0, 3, 4).reshape(bsz, n_heads, n_tok, dh)
    return o, s_fin


def _retention_groupnorm(o, g, b, dtype):
    mu = jnp.mean(o, axis=-1, keepdims=True)
    var = jnp.mean(jnp.square(o - mu), axis=-1, keepdims=True)
    on = _from_heads((o - mu) * lax.rsqrt(var + LN_EPS))
    return (on * g.astype(jnp.float32) + b.astype(jnp.float32)).astype(dtype)


def _retention_mixer(lat, ctx, dec_f, dec_b, gn_g, gn_b, cos, sin, with_ctx):
    ql, kl, vl, gl = lat
    qc, kc, vc, gc = ctx
    f32 = jnp.float32
    ql_h = _apply_rope(_to_heads(ql, N_GROUP_HEADS), cos, sin).astype(f32)
    kl_h = (_apply_rope(_to_heads(kl, N_GROUP_HEADS), cos, sin) * ATTN_SCALE).astype(f32)
    vl_h = _to_heads(vl, N_GROUP_HEADS).astype(f32)
    qc_h = _to_heads(qc, N_GROUP_HEADS).astype(f32)
    kc_h = (_to_heads(kc, N_GROUP_HEADS) * ATTN_SCALE).astype(f32)
    vc_h = _to_heads(vc, N_GROUP_HEADS).astype(f32)
    lg_f = jax.nn.log_sigmoid(dec_f.astype(f32))
    lg_b = jax.nn.log_sigmoid(dec_b.astype(f32))
    zero = jnp.zeros((ql.shape[0], N_GROUP_HEADS, HEAD_DIM, HEAD_DIM), f32)

    def rev(a):
        return jnp.flip(a, axis=2)

    oc_f, sc_f = _chunk_retention(qc_h, kc_h, vc_h, lg_f, zero)
    oc_b, sc_b = _chunk_retention(rev(qc_h), rev(kc_h), rev(vc_h), lg_b, zero)
    ol_f, _ = _chunk_retention(ql_h, kl_h, vl_h, lg_f, sc_f)
    ol_b, _ = _chunk_retention(rev(ql_h), rev(kl_h), rev(vl_h), lg_b, sc_b)
    y_lat = _retention_groupnorm(ol_f + rev(ol_b), gn_g, gn_b, gl.dtype) * jax.nn.silu(gl)
    y_ctx = None
    if with_ctx:
        y_ctx = _retention_groupnorm(oc_f + rev(oc_b), gn_g, gn_b, gc.dtype) * jax.nn.silu(gc)
    return y_lat, y_ctx


def _na_mixer(lat, ctx, bias_tab, with_ctx):
    ql, kl, vl, gl = lat
    qc, kc, vc, gc = ctx
    bsz, n_tok, _ = ql.shape
    rows = n_tok // GRID_W
    kr = min(NA_WIN_R, rows)
    q_g = _to_heads(ql, N_GROUP_HEADS).reshape(bsz, N_GROUP_HEADS, rows, GRID_W, HEAD_DIM)
    k_g = _to_heads(kl, N_GROUP_HEADS).reshape(bsz, N_GROUP_HEADS, rows, GRID_W, HEAD_DIM)
    v_g = _to_heads(vl, N_GROUP_HEADS).reshape(bsz, N_GROUP_HEADS, rows, GRID_W, HEAD_DIM)
    kc_h = _to_heads(kc, N_GROUP_HEADS)
    vc_h = _to_heads(vc, N_GROUP_HEADS)
    r = jnp.arange(rows)
    r_start = jnp.clip(r - kr // 2, 0, rows - kr)
    key_rows = r_start[:, None] + jnp.arange(kr)[None, :]
    k_nb = k_g[:, :, key_rows]
    v_nb = v_g[:, :, key_rows]
    s_nb = jnp.einsum('bhrqd,bhrjkd->bhrqjk', q_g, k_nb).astype(jnp.float32) * ATTN_SCALE
    col = jnp.arange(GRID_W)
    c_start = jnp.clip(col - NA_WIN_C // 2, 0, GRID_W - NA_WIN_C)
    in_win = (col[None, :] >= c_start[:, None]) & (col[None, :] < c_start[:, None] + NA_WIN_C)
    off_r = key_rows - r[:, None] + (NA_WIN_R - 1)
    off_c = jnp.clip(col[None, :] - col[:, None], -(NA_WIN_C - 1), NA_WIN_C - 1) + (NA_WIN_C - 1)
    bias = bias_tab[:, off_r[:, None, :, None], off_c[None, :, None, :]]
    s_nb = s_nb + bias[None].astype(jnp.float32)
    s_nb = jnp.where(in_win[None, None, None, :, None, :], s_nb, MASK_VALUE)
    s_ctx = jnp.einsum('bhrqd,bhsd->bhrqs', q_g, kc_h).astype(jnp.float32) * ATTN_SCALE
    n_nb = kr * GRID_W
    s = jnp.concatenate([s_nb.reshape(bsz, N_GROUP_HEADS, rows, GRID_W, n_nb), s_ctx], axis=-1)
    p = jax.nn.softmax(s, axis=-1).astype(vl.dtype)
    p_nb = p[..., :n_nb].reshape(bsz, N_GROUP_HEADS, rows, GRID_W, kr, GRID_W)
    p_ctx = p[..., n_nb:]
    o = (jnp.einsum('bhrqjk,bhrjkd->bhrqd', p_nb, v_nb)
         + jnp.einsum('bhrqs,bhsd->bhrqd', p_ctx, vc_h))
    o = o.reshape(bsz, N_GROUP_HEADS, n_tok, HEAD_DIM)
    y_lat = _from_heads(o) * jax.nn.silu(gl)
    y_ctx = None
    if with_ctx:
        n_ctx = qc.shape[1]
        qc_h = _to_heads(qc, N_GROUP_HEADS).reshape(bsz, N_GROUP_HEADS, 1, n_ctx, HEAD_DIM)
        oc = _softmax_attend(qc_h, kc_h, vc_h).reshape(bsz, N_GROUP_HEADS, n_ctx, HEAD_DIM)
        y_ctx = _from_heads(oc) * jax.nn.silu(gc)
    return y_lat, y_ctx


def setup_inputs(seed: int = 0) -> dict:
    key = jax.random.key(seed)
    ks = jax.random.split(key, 21)
    f32 = jnp.float32

    def nrm(k, shape, s):
        return s * jax.random.normal(k, shape, f32)

    decay_base = jnp.log(2.0 ** (RET_BASE_EXP + jnp.arange(N_GROUP_HEADS, dtype=f32)) - 1.0)
    return {
        'x': nrm(ks[0], (BATCH, SEQ, D_MODEL), 1.0),
        'c': nrm(ks[1], (BATCH, D_MODEL), 1.0),
        'ctx': nrm(ks[2], (BATCH, CTX_LEN, D_MODEL), 1.0),
        'c_ctx': nrm(ks[3], (D_MODEL,), 1.0),
        'ada_w': nrm(ks[4], (DEPTH, D_MODEL, 3 * D_MODEL), 0.5 * D_MODEL ** -0.5),
        'ada_b': nrm(ks[5], (DEPTH, 3 * D_MODEL), 0.01),
        'norm_g': 1.0 + nrm(ks[6], (DEPTH, D_MODEL), 0.02),
        'w_in': nrm(ks[7], (DEPTH, D_MODEL, P_TOTAL), D_MODEL ** -0.5),
        'conv_w': nrm(ks[8], (DEPTH, CONV_WIDTH, GROUP_W), CONV_WIDTH ** -0.5),
        'conv_b': nrm(ks[9], (DEPTH, GROUP_W), 0.01),
        'conv_ln_g': 1.0 + nrm(ks[10], (DEPTH, GROUP_W), 0.02),
        'conv_ln_b': nrm(ks[11], (DEPTH, GROUP_W), 0.01),
        'gqa_qn_g': 1.0 + nrm(ks[12], (DEPTH, HEAD_DIM), 0.02),
        'gqa_kn_g': 1.0 + nrm(ks[13], (DEPTH, HEAD_DIM), 0.02),
        'ret_decay_fwd': decay_base[None, :] + nrm(ks[14], (DEPTH, N_GROUP_HEADS), 0.1),
        'ret_decay_bwd': decay_base[None, :] + nrm(ks[15], (DEPTH, N_GROUP_HEADS), 0.1),
        'ret_gn_g': 1.0 + nrm(ks[16], (DEPTH, GROUP_W), 0.02),
        'ret_gn_b': nrm(ks[17], (DEPTH, GROUP_W), 0.01),
        'na_bias': nrm(ks[18], (DEPTH, N_GROUP_HEADS, 2 * NA_WIN_R - 1, 2 * NA_WIN_C - 1), 0.02),
        'w_out': nrm(ks[19], (DEPTH, MIX_W, D_MODEL), MIX_W ** -0.5),
        'final_g': 1.0 + nrm(ks[20], (D_MODEL,), 0.02),
    }


def reference(x, c, ctx, c_ctx, ada_w, ada_b, norm_g, w_in, conv_w, conv_b, conv_ln_g, conv_ln_b,
              gqa_qn_g, gqa_kn_g, ret_decay_fwd, ret_decay_bwd, ret_gn_g, ret_gn_b, na_bias,
              w_out, final_g):
    n_tok = x.shape[1]
    cos, sin = _axial_rope_tables(n_tok)
    h_ctx = ctx
    for l in range(DEPTH):
        with_ctx = l < DEPTH - 1
        shift, scale, gate = jnp.split(jax.nn.silu(c) @ ada_w[l] + ada_b[l], 3, axis=-1)
        shift_c, scale_c, gate_c = jnp.split(jax.nn.silu(c_ctx) @ ada_w[l] + ada_b[l], 3, axis=-1)
        hl = _rmsnorm(x, norm_g[l]) * (1.0 + scale[:, None, :]) + shift[:, None, :]
        hc = _rmsnorm(h_ctx, norm_g[l]) * (1.0 + scale_c) + shift_c
        pl = _split_cols(hl @ w_in[l])
        pc = _split_cols(hc @ w_in[l])
        ya_l = _conv_mixer(pl[0], pl[1], conv_w[l], conv_b[l], conv_ln_g[l], conv_ln_b[l])
        yb_l, yb_c = _gqa_mixer(pl[2:6], pc[2:6], gqa_qn_g[l], gqa_kn_g[l], cos, sin, with_ctx)
        yc_l, yc_c = _retention_mixer(pl[6:10], pc[6:10], ret_decay_fwd[l], ret_decay_bwd[l],
                                      ret_gn_g[l], ret_gn_b[l], cos, sin, with_ctx)
        yd_l, yd_c = _na_mixer(pl[10:14], pc[10:14], na_bias[l], with_ctx)
        y_l = jnp.concatenate([ya_l, yb_l, yc_l, yd_l], axis=-1)
        x = x + gate[:, None, :] * (y_l @ w_out[l])
        if with_ctx:
            ya_c = _conv_mixer(pc[0], pc[1], conv_w[l], conv_b[l], conv_ln_g[l], conv_ln_b[l])
            y_c = jnp.concatenate([ya_c, yb_c, yc_c, yd_c], axis=-1)
            h_ctx = h_ctx + gate_c * (y_c @ w_out[l])
    return _rmsnorm(x, final_g)
```

```python
import functools

import jax
import jax.numpy as jnp
from jax import lax
from jax.experimental import pallas as pl
from jax.experimental.pallas import tpu as pltpu

F32 = jnp.float32
BF16 = jnp.bfloat16

D_MODEL = 4096
BATCH = 4
SEQ = 2048
DEPTH = 2
GRID_W = 64
GRID_H = SEQ // GRID_W
CTX_LEN = 256
GROUP_W = 1024
HEAD_DIM = 128
N_HEADS = GROUP_W // HEAD_DIM
KV_HEADS = 2
Q_PER_KV = N_HEADS // KV_HEADS
CONV_WIDTH = 31
CONV_HALF = CONV_WIDTH // 2
CHUNK = 128
NA_WIN_R = 8
NA_WIN_C = 16
ROPE_THETA = 10000.0
NORM_EPS = 1e-6
LN_EPS = 1e-5
ATTN_SCALE = HEAD_DIM ** -0.5
MASK_VALUE = -1e30
P_TOTAL = 13 * GROUP_W + 2 * KV_HEADS * HEAD_DIM

COL_A_GLU = 0
COL_A_GATE = 16
COL_B_Q = 24
COL_B_K = 32
COL_B_V = 34
COL_B_GATE = 36
COL_C_Q = 44
COL_C_K = 52
COL_C_V = 60
COL_C_GATE = 68
COL_D_Q = 76
COL_D_K = 84
COL_D_V = 92
COL_D_GATE = 100

MOD_ROWS = 8
MOD_CTX_ROW = BATCH
VMEM_LIMIT = 56 * 1024 * 1024

NA_PATTERNS = 8
NA_KEYS = NA_WIN_R * GRID_W


def _params(sem):
    return pltpu.CompilerParams(dimension_semantics=sem, vmem_limit_bytes=VMEM_LIMIT)


def _silu(v):
    return v * jax.nn.sigmoid(v)


def _mod_kernel(c_ref, w_ref, b_ref, o_ref):
    a = _silu(c_ref[...]).astype(BF16)
    o_ref[...] = jnp.dot(a, w_ref[...].astype(BF16), preferred_element_type=F32) + b_ref[...]


def _ada_table(c8, ada_w, ada_b3, l):
    tn = 512
    n = 3 * D_MODEL
    return pl.pallas_call(
        _mod_kernel,
        grid=(n // tn,),
        in_specs=[
            pl.BlockSpec((MOD_ROWS, D_MODEL), lambda j: (0, 0)),
            pl.BlockSpec((None, D_MODEL, tn), lambda j: (l, 0, j)),
            pl.BlockSpec((None, 1, tn), lambda j: (l, 0, j)),
        ],
        out_specs=pl.BlockSpec((MOD_ROWS, tn), lambda j: (0, j)),
        out_shape=jax.ShapeDtypeStruct((MOD_ROWS, n), F32),
        compiler_params=_params(("arbitrary",)),
        name="ada_table",
    )(c8, ada_w, ada_b3)


def _normmod_kernel(x_ref, g_ref, shift_ref, scale_ref, o_ref, *, tiles_per_row, fixed_row):
    if fixed_row is None:
        r = pl.program_id(0) // tiles_per_row
    else:
        r = fixed_row
    x = x_ref[...]
    y = x * lax.rsqrt(jnp.mean(x * x, axis=-1, keepdims=True) + NORM_EPS)
    y = y * g_ref[...]
    shift = shift_ref[pl.ds(r, 1), :]
    scale = scale_ref[pl.ds(r, 1), :]
    o_ref[...] = (y * (1.0 + scale) + shift).astype(o_ref.dtype)


def _norm_modulate(x2, norm_g3, l, mod, rows_per_mod, fixed_row):
    m = x2.shape[0]
    tm = 256
    tiles_per_row = None if rows_per_mod is None else rows_per_mod // tm
    kern = functools.partial(_normmod_kernel, tiles_per_row=tiles_per_row, fixed_row=fixed_row)
    return pl.pallas_call(
        kern,
        grid=(m // tm,),
        in_specs=[
            pl.BlockSpec((tm, D_MODEL), lambda i: (i, 0)),
            pl.BlockSpec((None, 1, D_MODEL), lambda i: (l, 0, 0)),
            pl.BlockSpec((MOD_ROWS, D_MODEL), lambda i: (0, 0)),
            pl.BlockSpec((MOD_ROWS, D_MODEL), lambda i: (0, 1)),
        ],
        out_specs=pl.BlockSpec((tm, D_MODEL), lambda i: (i, 0)),
        out_shape=jax.ShapeDtypeStruct((m, D_MODEL), BF16),
        compiler_params=_params(("arbitrary",)),
        name="norm_modulate",
    )(x2, norm_g3, mod, mod)


def _mm_kernel(a_ref, w_ref, o_ref):
    o_ref[...] = jnp.dot(a_ref[...], w_ref[...], preferred_element_type=F32)


def _in_proj(h, w_in_bf, l):
    m = h.shape[0]
    tm = min(m, 1024)
    tn = 512
    return pl.pallas_call(
        _mm_kernel,
        grid=(m // tm, P_TOTAL // tn),
        in_specs=[
            pl.BlockSpec((tm, D_MODEL), lambda i, j: (i, 0)),
            pl.BlockSpec((None, D_MODEL, tn), lambda i, j: (l, 0, j)),
        ],
        out_specs=pl.BlockSpec((tm, tn), lambda i, j: (i, j)),
        out_shape=jax.ShapeDtypeStruct((m, P_TOTAL), F32),
        compiler_params=_params(("arbitrary", "arbitrary")),
        name="in_proj",
    )(h, w_in_bf)


def _out_kernel(ya_ref, yb_ref, yc_ref, yd_ref, w_ref, x_ref, gate_ref, o_ref, *,
                tiles_per_row, fixed_row):
    if fixed_row is None:
        r = pl.program_id(0) // tiles_per_row
    else:
        r = fixed_row
    acc = jnp.dot(ya_ref[...], w_ref[0 * GROUP_W:1 * GROUP_W, :], preferred_element_type=F32)
    acc += jnp.dot(yb_ref[...], w_ref[1 * GROUP_W:2 * GROUP_W, :], preferred_element_type=F32)
    acc += jnp.dot(yc_ref[...], w_ref[2 * GROUP_W:3 * GROUP_W, :], preferred_element_type=F32)
    acc += jnp.dot(yd_ref[...], w_ref[3 * GROUP_W:4 * GROUP_W, :], preferred_element_type=F32)
    gate = gate_ref[pl.ds(r, 1), :]
    o_ref[...] = x_ref[...] + gate * acc


def _out_proj(ys, w_out_bf, l, x2, mod, rows_per_mod, fixed_row):
    m = x2.shape[0]
    tm = min(m, 1024)
    tn = 512
    tiles_per_row = None if rows_per_mod is None else rows_per_mod // tm
    kern = functools.partial(_out_kernel, tiles_per_row=tiles_per_row, fixed_row=fixed_row)
    y_spec = pl.BlockSpec((tm, GROUP_W), lambda i, j: (i, 0))
    gate_col0 = 2 * D_MODEL // tn
    return pl.pallas_call(
        kern,
        grid=(m // tm, D_MODEL // tn),
        in_specs=[
            y_spec, y_spec, y_spec, y_spec,
            pl.BlockSpec((None, D_MODEL, tn), lambda i, j: (l, 0, j)),
            pl.BlockSpec((tm, tn), lambda i, j: (i, j)),
            pl.BlockSpec((MOD_ROWS, tn), lambda i, j: (0, gate_col0 + j)),
        ],
        out_specs=pl.BlockSpec((tm, tn), lambda i, j: (i, j)),
        out_shape=jax.ShapeDtypeStruct((m, D_MODEL), F32),
        compiler_params=_params(("arbitrary", "arbitrary")),
        name="out_proj",
    )(*ys, w_out_bf, x2, mod)


def _final_norm_kernel(x_ref, g_ref, o_ref):
    x = x_ref[...]
    o_ref[...] = x * lax.rsqrt(jnp.mean(x * x, axis=-1, keepdims=True) + NORM_EPS) * g_ref[...]


def _final_norm(x2, g2):
    m = x2.shape[0]
    tm = 256
    return pl.pallas_call(
        _final_norm_kernel,
        grid=(m // tm,),
        in_specs=[pl.BlockSpec((tm, D_MODEL), lambda i: (i, 0)),
                  pl.BlockSpec((1, D_MODEL), lambda i: (0, 0))],
        out_specs=pl.BlockSpec((tm, D_MODEL), lambda i: (i, 0)),
        out_shape=jax.ShapeDtypeStruct((m, D_MODEL), F32),
        compiler_params=_params(("arbitrary",)),
        name="final_norm",
    )(x2, g2)


CONV_HALO = 16
CONV_SUB = 64


def _conv_kernel(cur_ref, prev_ref, next_ref, gate_ref, w_ref, b_ref, lng_ref, lnb_ref, o_ref,
                 u_ref, c_ref, *, tile, n_tiles):
    t = pl.program_id(1)

    def glu(ref):
        return ref[:, :GROUP_W] * jax.nn.sigmoid(ref[:, GROUP_W:])

    u_ref[CONV_HALO:CONV_HALO + tile, :] = glu(cur_ref)
    u_ref[0:CONV_HALO, :] = jnp.where(t > 0, glu(prev_ref), 0.0)
    u_ref[CONV_HALO + tile:, :] = jnp.where(t < n_tiles - 1, glu(next_ref), 0.0)

    base = CONV_HALO - CONV_HALF
    for s in range(tile // CONV_SUB):
        for c in range(GROUP_W // HEAD_DIM):
            lanes = slice(c * HEAD_DIM, (c + 1) * HEAD_DIM)
            acc = jnp.zeros((CONV_SUB, HEAD_DIM), F32)
            for k in range(CONV_WIDTH):
                r0 = s * CONV_SUB + base + k
                acc = acc + w_ref[k:k + 1, lanes] * u_ref[r0:r0 + CONV_SUB, lanes]
            c_ref[s * CONV_SUB:(s + 1) * CONV_SUB, lanes] = acc + b_ref[:, lanes]

    v = c_ref[...]
    mu = jnp.mean(v, axis=-1, keepdims=True)
    var = jnp.mean(jnp.square(v - mu), axis=-1, keepdims=True)
    y = (v - mu) * lax.rsqrt(var + LN_EPS) * lng_ref[...] + lnb_ref[...]
    o_ref[...] = (_silu(y) * _silu(gate_ref[...])).astype(o_ref.dtype)


def _conv_mixer(p, seq_len, conv_w, conv_b3, ln_g3, ln_b3, l):
    tile = 256
    n_tiles = seq_len // tile
    n_seq = p.shape[0] // seq_len
    halo_per_tile = tile // CONV_HALO
    last_halo = p.shape[0] // CONV_HALO - 1
    kern = functools.partial(_conv_kernel, tile=tile, n_tiles=n_tiles)
    vec = pl.BlockSpec((None, 1, GROUP_W), lambda b, t: (l, 0, 0))
    return pl.pallas_call(
        kern,
        grid=(n_seq, n_tiles),
        in_specs=[
            pl.BlockSpec((tile, 2 * GROUP_W), lambda b, t: (b * n_tiles + t, 0)),
            pl.BlockSpec((CONV_HALO, 2 * GROUP_W),
                         lambda b, t: (jnp.maximum((b * n_tiles + t) * halo_per_tile - 1, 0), 0)),
            pl.BlockSpec((CONV_HALO, 2 * GROUP_W),
                         lambda b, t: (jnp.minimum((b * n_tiles + t + 1) * halo_per_tile, last_halo), 0)),
            pl.BlockSpec((tile, GROUP_W), lambda b, t: (b * n_tiles + t, COL_A_GATE // 8)),
            pl.BlockSpec((None, CONV_WIDTH, GROUP_W), lambda b, t: (l, 0, 0)),
            vec, vec, vec,
        ],
        out_specs=pl.BlockSpec((tile, GROUP_W), lambda b, t: (b * n_tiles + t, 0)),
        out_shape=jax.ShapeDtypeStruct((p.shape[0], GROUP_W), BF16),
        scratch_shapes=[pltpu.VMEM((tile + 2 * CONV_HALO, GROUP_W), F32),
                        pltpu.VMEM((tile, GROUP_W), F32)],
        compiler_params=_params(("arbitrary", "arbitrary")),
        name="conv_mixer",
    )(p, p, p, p, conv_w, conv_b3, ln_g3, ln_b3)


def _head_rmsnorm(v, g):
    return v * lax.rsqrt(jnp.mean(v * v, axis=-1, keepdims=True) + NORM_EPS) * g


def _rope(v, cos, sin_a, sin_b):
    return (v * cos + pltpu.roll(v, HEAD_DIM - HEAD_DIM // 4, 1) * sin_a
            + pltpu.roll(v, HEAD_DIM // 4, 1) * sin_b)


def _dot_nt(a, b):
    return lax.dot_general(a, b, (((1,), (1,)), ((), ())), preferred_element_type=F32)


def _softmax_pv(s_list, v_list):
    m = s_list[0].max(axis=-1, keepdims=True)
    for s in s_list[1:]:
        m = jnp.maximum(m, s.max(axis=-1, keepdims=True))
    den = None
    acc = None
    for s, v in zip(s_list, v_list):
        e = jnp.exp(s - m)
        d = e.sum(axis=-1, keepdims=True)
        o = jnp.dot(e.astype(BF16), v, preferred_element_type=F32)
        den = d if den is None else den + d
        acc = o if acc is None else acc + o
    return acc / den


def _gqa_kernel(*refs, tq, with_ctx):
    if with_ctx:
        (q_ref, kl_ref, vl_ref, kc_ref, vc_ref, g_ref, qc_ref, gc_ref,
         cos_q, sa_q, sb_q, cos_k, sa_k, sb_k, qn_ref, kn_ref,
         o_ref, oc_ref, k_sc, v_sc) = refs
    else:
        (q_ref, kl_ref, vl_ref, kc_ref, vc_ref, g_ref,
         cos_q, sa_q, sb_q, cos_k, sa_k, sb_k, qn_ref, kn_ref,
         o_ref, k_sc, v_sc) = refs
    qi = pl.program_id(2)
    qn = qn_ref[...]
    kn = kn_ref[...]

    @pl.when(qi == 0)
    def _():
        k_sc[0:CTX_LEN, :] = _head_rmsnorm(kc_ref[...], kn).astype(BF16)
        k_sc[CTX_LEN:, :] = _rope(_head_rmsnorm(kl_ref[...], kn),
                                  cos_k[...], sa_k[...], sb_k[...]).astype(BF16)
        v_sc[0:CTX_LEN, :] = vc_ref[...].astype(BF16)
        v_sc[CTX_LEN:, :] = vl_ref[...].astype(BF16)
        if with_ctx:
            kc = k_sc[0:CTX_LEN, :]
            vc = v_sc[0:CTX_LEN, :]
            for h in range(Q_PER_KV):
                lanes = slice(h * HEAD_DIM, (h + 1) * HEAD_DIM)
                qh = _head_rmsnorm(qc_ref[:, lanes], qn).astype(BF16)
                s = _dot_nt(qh, kc) * ATTN_SCALE
                o = _softmax_pv([s], [vc])
                oc_ref[:, lanes] = (o * _silu(gc_ref[:, lanes])).astype(oc_ref.dtype)

    k_all = k_sc[...]
    v_all = v_sc[...]
    for h in range(Q_PER_KV):
        lanes = slice(h * HEAD_DIM, (h + 1) * HEAD_DIM)
        qh = _rope(_head_rmsnorm(q_ref[:, lanes], qn), cos_q[...], sa_q[...], sb_q[...]).astype(BF16)
        s = _dot_nt(qh, k_all) * ATTN_SCALE
        o = _softmax_pv([s], [v_all])
        o_ref[:, lanes] = (o * _silu(g_ref[:, lanes])).astype(o_ref.dtype)


def _gqa_mixer(p_lat, p_ctx, rope_tabs, qn3, kn3, l, with_ctx):
    tq = 256
    nq = SEQ // tq
    qw = Q_PER_KV * HEAD_DIM
    cos, sin_a, sin_b = rope_tabs
    kern = functools.partial(_gqa_kernel, tq=tq, with_ctx=with_ctx)
    in_specs = [
        pl.BlockSpec((tq, qw), lambda b, kv, qi: (b * nq + qi, COL_B_Q // Q_PER_KV + kv)),
        pl.BlockSpec((SEQ, HEAD_DIM), lambda b, kv, qi: (b, COL_B_K + kv)),
        pl.BlockSpec((SEQ, HEAD_DIM), lambda b, kv, qi: (b, COL_B_V + kv)),
        pl.BlockSpec((CTX_LEN, HEAD_DIM), lambda b, kv, qi: (b, COL_B_K + kv)),
        pl.BlockSpec((CTX_LEN, HEAD_DIM), lambda b, kv, qi: (b, COL_B_V + kv)),
        pl.BlockSpec((tq, qw), lambda b, kv, qi: (b * nq + qi, COL_B_GATE // Q_PER_KV + kv)),
    ]
    args = [p_lat, p_lat, p_lat, p_ctx, p_ctx, p_lat]
    if with_ctx:
        in_specs += [
            pl.BlockSpec((CTX_LEN, qw), lambda b, kv, qi: (b, COL_B_Q // Q_PER_KV + kv)),
            pl.BlockSpec((CTX_LEN, qw), lambda b, kv, qi: (b, COL_B_GATE // Q_PER_KV + kv)),
        ]
        args += [p_ctx, p_ctx]
    tab_q = pl.BlockSpec((tq, HEAD_DIM), lambda b, kv, qi: (qi, 0))
    tab_k = pl.BlockSpec((SEQ, HEAD_DIM), lambda b, kv, qi: (0, 0))
    gvec = pl.BlockSpec((None, 1, HEAD_DIM), lambda b, kv, qi: (l, 0, 0))
    in_specs += [tab_q, tab_q, tab_q, tab_k, tab_k, tab_k, gvec, gvec]
    args += [cos, sin_a, sin_b, cos, sin_a, sin_b, qn3, kn3]
    out_specs = [pl.BlockSpec((tq, qw), lambda b, kv, qi: (b * nq + qi, kv))]
    out_shape = [jax.ShapeDtypeStruct((BATCH * SEQ, GROUP_W), BF16)]
    if with_ctx:
        out_specs.append(pl.BlockSpec((CTX_LEN, qw), lambda b, kv, qi: (b, kv)))
        out_shape.append(jax.ShapeDtypeStruct((BATCH * CTX_LEN, GROUP_W), BF16))
    res = pl.pallas_call(
        kern,
        grid=(BATCH, KV_HEADS, nq),
        in_specs=in_specs,
        out_specs=out_specs,
        out_shape=out_shape,
        scratch_shapes=[pltpu.VMEM((CTX_LEN + SEQ, HEAD_DIM), BF16),
                        pltpu.VMEM((CTX_LEN + SEQ, HEAD_DIM), BF16)],
        compiler_params=_params(("arbitrary", "arbitrary", "arbitrary")),
        name="gqa_mixer",
    )(*args)
    return (res[0], res[1]) if with_ctx else (res[0], None)


def _log_sigmoid(v):
    return -(jnp.maximum(-v, 0.0) + jnp.log1p(jnp.exp(-jnp.abs(v))))


def _ret_kernel(*refs, with_ctx):
    if with_ctx:
        (ql_ref, kl_ref, vl_ref, gl_ref, qc_ref, kc_ref, vc_ref, gc_ref,
         cos_ref, sa_ref, sb_ref, decf_ref, decb_ref, gng_ref, gnb_ref,
         o_ref, oc_ref, q_sc, k_sc, v_sc, acc_sc) = refs
    else:
        (ql_ref, kl_ref, vl_ref, gl_ref, kc_ref, vc_ref,
         cos_ref, sa_ref, sb_ref, decf_ref, decb_ref, gng_ref, gnb_ref,
         o_ref, q_sc, k_sc, v_sc, acc_sc) = refs
        qc_ref = None

    n_ctx = CTX_LEN // CHUNK
    n_all = (CTX_LEN + SEQ) // CHUNK

    if with_ctx:
        q_sc[0:CTX_LEN, :] = qc_ref[...]
    q_sc[CTX_LEN:, :] = _rope(ql_ref[...], cos_ref[...], sa_ref[...], sb_ref[...])
    k_sc[0:CTX_LEN, :] = kc_ref[...] * ATTN_SCALE
    k_sc[CTX_LEN:, :] = _rope(kl_ref[...], cos_ref[...], sa_ref[...], sb_ref[...]) * ATTN_SCALE
    v_sc[0:CTX_LEN, :] = vc_ref[...].astype(BF16)
    v_sc[CTX_LEN:, :] = vl_ref[...].astype(BF16)

    lg_f = _log_sigmoid(decf_ref[...])
    lg_b = _log_sigmoid(decb_ref[...])
    ii = lax.broadcasted_iota(jnp.int32, (CHUNK, CHUNK), 0).astype(F32)
    jj = lax.broadcasted_iota(jnp.int32, (CHUNK, CHUNK), 1).astype(F32)
    dfw = ii - jj
    decay_f = jnp.where(dfw >= 0, jnp.exp(jnp.maximum(dfw, 0.0) * lg_f), 0.0)
    decay_b = jnp.where(dfw <= 0, jnp.exp(jnp.maximum(-dfw, 0.0) * lg_b), 0.0)
    idx = ii[:, 0:1]
    lgf_c = lg_f[:, 0:1]
    lgb_c = lg_b[:, 0:1]
    qdec_f = jnp.exp((idx + 1.0) * lgf_c)
    kdec_f = jnp.exp((CHUNK - 1.0 - idx) * lgf_c)
    cdec_f = jnp.exp(CHUNK * lgf_c)
    qdec_b = jnp.exp((CHUNK - idx) * lgb_c)
    kdec_b = jnp.exp(idx * lgb_c)
    cdec_b = jnp.exp(CHUNK * lgb_c)

    def chunk_step(c, s, decay, qdec, kdec, cdec, accumulate, need_out):
        rows = pl.ds(pl.multiple_of(c * CHUNK, CHUNK), CHUNK)
        kc = k_sc[rows, :]
        vc = v_sc[rows, :]
        if need_out:
            qc = q_sc[rows, :]
            att = _dot_nt(qc.astype(BF16), kc.astype(BF16)) * decay
            o = jnp.dot(att.astype(BF16), vc, preferred_element_type=F32)
            o = o + jnp.dot((qc * qdec).astype(BF16), s.astype(BF16), preferred_element_type=F32)
            if accumulate:
                acc_sc[rows, :] = acc_sc[rows, :] + o
            else:
                acc_sc[rows, :] = o
        kd_t = jnp.transpose(kc * kdec).astype(BF16)
        return s * cdec + jnp.dot(kd_t, vc, preferred_element_type=F32)

    s0 = jnp.zeros((HEAD_DIM, HEAD_DIM), F32)
    s = s0
    for c in range(n_ctx):
        s = chunk_step(c, s, decay_f, qdec_f, kdec_f, cdec_f, False, with_ctx)
    s = lax.fori_loop(
        n_ctx, n_all,
        lambda c, st: chunk_step(c, st, decay_f, qdec_f, kdec_f, cdec_f, False, True), s)
    s = s0
    for c in reversed(range(n_ctx)):
        s = chunk_step(c, s, decay_b, qdec_b, kdec_b, cdec_b, True, with_ctx)
    lax.fori_loop(
        0, n_all - n_ctx,
        lambda t, st: chunk_step(n_all - 1 - t, st, decay_b, qdec_b, kdec_b, cdec_b, True, True), s)

    def group_norm(o, gate):
        mu = jnp.mean(o, axis=-1, keepdims=True)
        var = jnp.mean(jnp.square(o - mu), axis=-1, keepdims=True)
        on = (o - mu) * lax.rsqrt(var + LN_EPS)
        return (on * gng_ref[...] + gnb_ref[...]) * _silu(gate)

    o_ref[...] = group_norm(acc_sc[CTX_LEN:, :], gl_ref[...]).astype(o_ref.dtype)
    if with_ctx:
        oc_ref[...] = group_norm(acc_sc[0:CTX_LEN, :], gc_ref[...]).astype(oc_ref.dtype)


def _retention_mixer(p_lat, p_ctx, rope_tabs, dec_f, dec_b, gn_g3, gn_b3, l, with_ctx):
    cos, sin_a, sin_b = rope_tabs
    kern = functools.partial(_ret_kernel, with_ctx=with_ctx)

    def col(c0, rows):
        return pl.BlockSpec((rows, HEAD_DIM), lambda b, h: (b, c0 + h))

    in_specs = [col(COL_C_Q, SEQ), col(COL_C_K, SEQ), col(COL_C_V, SEQ), col(COL_C_GATE, SEQ)]
    args = [p_lat, p_lat, p_lat, p_lat]
    if with_ctx:
        in_specs += [col(COL_C_Q, CTX_LEN), col(COL_C_K, CTX_LEN), col(COL_C_V, CTX_LEN),
                     col(COL_C_GATE, CTX_LEN)]
        args += [p_ctx, p_ctx, p_ctx, p_ctx]
    else:
        in_specs += [col(COL_C_K, CTX_LEN), col(COL_C_V, CTX_LEN)]
        args += [p_ctx, p_ctx]
    tab = pl.BlockSpec((SEQ, HEAD_DIM), lambda b, h: (0, 0))
    dec = pl.BlockSpec((None, None, 1, HEAD_DIM), lambda b, h: (l, h, 0, 0))
    gvec = pl.BlockSpec((None, 1, HEAD_DIM), lambda b, h: (l, 0, h))
    in_specs += [tab, tab, tab, dec, dec, gvec, gvec]
    args += [cos, sin_a, sin_b, dec_f, dec_b, gn_g3, gn_b3]
    out_specs = [pl.BlockSpec((SEQ, HEAD_DIM), lambda b, h: (b, h))]
    out_shape = [jax.ShapeDtypeStruct((BATCH * SEQ, GROUP_W), BF16)]
    if with_ctx:
        out_specs.append(pl.BlockSpec((CTX_LEN, HEAD_DIM), lambda b, h: (b, h)))
        out_shape.append(jax.ShapeDtypeStruct((BATCH * CTX_LEN, GROUP_W), BF16))
    n_rows = CTX_LEN + SEQ
    res = pl.pallas_call(
        kern,
        grid=(BATCH, N_HEADS),
        in_specs=in_specs,
        out_specs=out_specs,
        out_shape=out_shape,
        scratch_shapes=[pltpu.VMEM((n_rows, HEAD_DIM), F32),
                        pltpu.VMEM((n_rows, HEAD_DIM), F32),
                        pltpu.VMEM((n_rows, HEAD_DIM), BF16),
                        pltpu.VMEM((n_rows, HEAD_DIM), F32)],
        compiler_params=_params(("arbitrary", "arbitrary")),
        name="retention_mixer",
    )(*args)
    return (res[0], res[1]) if with_ctx else (res[0], None)


def _na_row_start(r):
    return min(max(r - NA_WIN_R // 2, 0), GRID_H - NA_WIN_R)


def _na_pattern_rows():
    top = list(range(NA_WIN_R // 2))
    bottom = list(range(GRID_H - NA_WIN_R // 2 + 1, GRID_H))
    return top + [NA_WIN_R // 2] + bottom


def _na_bias_kernel(tab_ref, o_ref):
    q = lax.broadcasted_iota(jnp.int32, (GRID_W, HEAD_DIM), 0)
    kcol = lax.broadcasted_iota(jnp.int32, (GRID_W, HEAD_DIM), 1) % GRID_W
    off_c = jnp.clip(kcol - q, -(NA_WIN_C - 1), NA_WIN_C - 1) + (NA_WIN_C - 1)
    lower_half = lax.broadcasted_iota(jnp.int32, (GRID_W, HEAD_DIM), 1) < GRID_W
    n_off_r = 2 * NA_WIN_R - 1
    planes = []
    for o in range(n_off_r):
        g = jnp.zeros((GRID_W, HEAD_DIM), F32)
        for c in range(2 * NA_WIN_C - 1):
            g = jnp.where(off_c == c, tab_ref[0, o, c], g)
        planes.append(g)
    for pat, r in enumerate(_na_pattern_rows()):
        r0 = _na_row_start(r)
        for jp in range(NA_WIN_R // 2):
            o_even = r0 + 2 * jp - r + (NA_WIN_R - 1)
            o_odd = o_even + 1
            o_ref[0, pat, :, jp * HEAD_DIM:(jp + 1) * HEAD_DIM] = jnp.where(
                lower_half, planes[o_even], planes[o_odd])


def _na_bias_table(na_bias_l):
    return pl.pallas_call(
        _na_bias_kernel,
        grid=(N_HEADS,),
        in_specs=[pl.BlockSpec((1, 2 * NA_WIN_R - 1, 2 * NA_WIN_C - 1), lambda h: (h, 0, 0),
                               memory_space=pltpu.SMEM)],
        out_specs=pl.BlockSpec((1, NA_PATTERNS, GRID_W, NA_KEYS), lambda h: (h, 0, 0, 0)),
        out_shape=jax.ShapeDtypeStruct((N_HEADS, NA_PATTERNS, GRID_W, NA_KEYS), F32),
        compiler_params=_params(("arbitrary",)),
        name="na_bias_table",
    )(na_bias_l)


def _na_kernel(*refs, with_ctx):
    if with_ctx:
        (q_ref, k_ref, v_ref, g_ref, qc_ref, kc_ref, vc_ref, gc_ref, bias_ref,
         o_ref, oc_ref, k_sc, v_sc) = refs
    else:
        (q_ref, k_ref, v_ref, g_ref, kc_ref, vc_ref, bias_ref, o_ref, k_sc, v_sc) = refs
    k_sc[...] = k_ref[...].astype(BF16)
    v_sc[...] = v_ref[...].astype(BF16)
    kc = kc_ref[...].astype(BF16)
    vc = vc_ref[...].astype(BF16)

    q_idx = lax.broadcasted_iota(jnp.int32, (GRID_W, NA_KEYS), 0)
    k_col = lax.broadcasted_iota(jnp.int32, (GRID_W, NA_KEYS), 1) % GRID_W
    c_start = jnp.clip(q_idx - NA_WIN_C // 2, 0, GRID_W - NA_WIN_C)
    in_win = (k_col >= c_start) & (k_col < c_start + NA_WIN_C)
    half = NA_WIN_R // 2

    def row_step(r, carry):
        r0 = jnp.clip(r - half, 0, GRID_H - NA_WIN_R)
        pat = jnp.where(r < half, r, jnp.where(r <= GRID_H - half, half, r - (GRID_H - NA_WIN_R)))
        rows = pl.ds(pl.multiple_of(r * GRID_W, GRID_W), GRID_W)
        keys = pl.ds(pl.multiple_of(r0 * GRID_W, GRID_W), NA_KEYS)
        qr = q_ref[rows, :].astype(BF16)
        s_nb = _dot_nt(qr, k_sc[keys, :]) * ATTN_SCALE + bias_ref[0, pat]
        s_nb = jnp.where(in_win, s_nb, MASK_VALUE)
        s_cx = _dot_nt(qr, kc) * ATTN_SCALE
        o = _softmax_pv([s_nb, s_cx], [v_sc[keys, :], vc])
        o_ref[rows, :] = (o * _silu(g_ref[rows, :])).astype(o_ref.dtype)
        return carry

    lax.fori_loop(0, GRID_H, row_step, 0)

    if with_ctx:
        s = _dot_nt(qc_ref[...].astype(BF16), kc) * ATTN_SCALE
        o = _softmax_pv([s], [vc])
        oc_ref[...] = (o * _silu(gc_ref[...])).astype(oc_ref.dtype)


def _na_mixer(p_lat, p_ctx, bias_tab, with_ctx):
    kern = functools.partial(_na_kernel, with_ctx=with_ctx)

    def col(c0, rows):
        return pl.BlockSpec((rows, HEAD_DIM), lambda b, h: (b, c0 + h))

    in_specs = [col(COL_D_Q, SEQ), col(COL_D_K, SEQ), col(COL_D_V, SEQ), col(COL_D_GATE, SEQ)]
    args = [p_lat, p_lat, p_lat, p_lat]
    if with_ctx:
        in_specs += [col(COL_D_Q, CTX_LEN), col(COL_D_K, CTX_LEN), col(COL_D_V, CTX_LEN),
                     col(COL_D_GATE, CTX_LEN)]
        args += [p_ctx, p_ctx, p_ctx, p_ctx]
    else:
        in_specs += [col(COL_D_K, CTX_LEN), col(COL_D_V, CTX_LEN)]
        args += [p_ctx, p_ctx]
    in_specs.append(pl.BlockSpec((1, NA_PATTERNS, GRID_W, NA_KEYS), lambda b, h: (h, 0, 0, 0)))
    args.append(bias_tab)
    out_specs = [pl.BlockSpec((SEQ, HEAD_DIM), lambda b, h: (b, h))]
    out_shape = [jax.ShapeDtypeStruct((BATCH * SEQ, GROUP_W), BF16)]
    if with_ctx:
        out_specs.append(pl.BlockSpec((CTX_LEN, HEAD_DIM), lambda b, h: (b, h)))
        out_shape.append(jax.ShapeDtypeStruct((BATCH * CTX_LEN, GROUP_W), BF16))
    res = pl.pallas_call(
        kern,
        grid=(BATCH, N_HEADS),
        in_specs=in_specs,
        out_specs=out_specs,
        out_shape=out_shape,
        scratch_shapes=[pltpu.VMEM((SEQ, HEAD_DIM), BF16), pltpu.VMEM((SEQ, HEAD_DIM), BF16)],
        compiler_params=_params(("arbitrary", "arbitrary")),
        name="na_mixer",
    )(*args)
    return (res[0], res[1]) if with_ctx else (res[0], None)


def _rope_tables():
    t = jnp.arange(SEQ)
    row = (t // GRID_W).astype(F32)
    col = (t % GRID_W).astype(F32)
    half = HEAD_DIM // 2
    inv_freq = ROPE_THETA ** (-jnp.arange(0, half, 2, dtype=F32) / half)
    ang_r = row[:, None] * inv_freq[None, :]
    ang_c = col[:, None] * inv_freq[None, :]
    ang = jnp.concatenate([ang_r, ang_r, ang_c, ang_c], axis=-1)
    cos, sin = jnp.cos(ang), jnp.sin(ang)
    quarter = (jnp.arange(HEAD_DIM) // (HEAD_DIM // 4)) % 2
    sin_a = jnp.where(quarter[None, :] == 0, -sin, 0.0)
    sin_b = jnp.where(quarter[None, :] == 1, sin, 0.0)
    return cos, sin_a, sin_b


def kernel(x, c, ctx, c_ctx, ada_w, ada_b, norm_g, w_in, conv_w, conv_b, conv_ln_g, conv_ln_b,
           gqa_qn_g, gqa_kn_g, ret_decay_fwd, ret_decay_bwd, ret_gn_g, ret_gn_b, na_bias,
           w_out, final_g):
    rope_tabs = _rope_tables()
    x_lat = x.reshape(BATCH * SEQ, D_MODEL)
    x_ctx = ctx.reshape(BATCH * CTX_LEN, D_MODEL)
    c8 = jnp.concatenate(
        [c, c_ctx[None, :], jnp.zeros((MOD_ROWS - BATCH - 1, D_MODEL), F32)], axis=0)
    w_in_bf = w_in.astype(BF16)
    w_out_bf = w_out.astype(BF16)

    def vec3(a):
        return a.reshape(DEPTH, 1, a.shape[-1])

    ada_b3, norm_g3 = vec3(ada_b), vec3(norm_g)
    conv_b3, ln_g3, ln_b3 = vec3(conv_b), vec3(conv_ln_g), vec3(conv_ln_b)
    qn3, kn3 = vec3(gqa_qn_g), vec3(gqa_kn_g)
    gn_g3, gn_b3 = vec3(ret_gn_g), vec3(ret_gn_b)
    dec_f = jnp.broadcast_to(ret_decay_fwd[:, :, None, None], (DEPTH, N_HEADS, 1, HEAD_DIM))
    dec_b = jnp.broadcast_to(ret_decay_bwd[:, :, None, None], (DEPTH, N_HEADS, 1, HEAD_DIM))

    for l in range(DEPTH):
        with_ctx = l < DEPTH - 1
        mod = _ada_table(c8, ada_w, ada_b3, l)
        h_lat = _norm_modulate(x_lat, norm_g3, l, mod, SEQ, None)
        h_ctx = _norm_modulate(x_ctx, norm_g3, l, mod, None, MOD_CTX_ROW)
        p_lat = _in_proj(h_lat, w_in_bf, l)
        p_ctx = _in_proj(h_ctx, w_in_bf, l)

        ya_l = _conv_mixer(p_lat, SEQ, conv_w, conv_b3, ln_g3, ln_b3, l)
        yb_l, yb_c = _gqa_mixer(p_lat, p_ctx, rope_tabs, qn3, kn3, l, with_ctx)
        yc_l, yc_c = _retention_mixer(p_lat, p_ctx, rope_tabs, dec_f, dec_b, gn_g3, gn_b3, l,
                                      with_ctx)
        bias_tab = _na_bias_table(na_bias[l])
        yd_l, yd_c = _na_mixer(p_lat, p_ctx, bias_tab, with_ctx)

        x_new = _out_proj((ya_l, yb_l, yc_l, yd_l), w_out_bf, l, x_lat, mod, SEQ, None)
        if with_ctx:
            ya_c = _conv_mixer(p_ctx, CTX_LEN, conv_w, conv_b3, ln_g3, ln_b3, l)
            x_ctx = _out_proj((ya_c, yb_c, yc_c, yd_c), w_out_bf, l, x_ctx, mod, None, MOD_CTX_ROW)
        x_lat = x_new

    out = _final_norm(x_lat, final_g.reshape(1, D_MODEL))
    return out.reshape(BATCH, SEQ, D_MODEL)
```

```python
import functools

import jax
import jax.numpy as jnp
from jax import lax
from jax.experimental import pallas as pl
from jax.experimental.pallas import tpu as pltpu

F32 = jnp.float32
BF16 = jnp.bfloat16

D_MODEL = 4096
BATCH = 4
SEQ = 2048
DEPTH = 2
GRID_W = 64
GRID_H = SEQ // GRID_W
CTX_LEN = 256
GROUP_W = 1024
HEAD_DIM = 128
N_HEADS = GROUP_W // HEAD_DIM
KV_HEADS = 2
Q_PER_KV = N_HEADS // KV_HEADS
CONV_WIDTH = 31
CONV_HALF = CONV_WIDTH // 2
CHUNK = 128
NA_WIN_R = 8
NA_WIN_C = 16
ROPE_THETA = 10000.0
NORM_EPS = 1e-6
LN_EPS = 1e-5
ATTN_SCALE = HEAD_DIM ** -0.5
MASK_VALUE = -1e30
P_TOTAL = 13 * GROUP_W + 2 * KV_HEADS * HEAD_DIM

COL_A_GLU = 0
COL_A_GATE = 16
COL_B_Q = 24
COL_B_K = 32
COL_B_V = 34
COL_B_GATE = 36
COL_C_Q = 44
COL_C_K = 52
COL_C_V = 60
COL_C_GATE = 68
COL_D_Q = 76
COL_D_K = 84
COL_D_V = 92
COL_D_GATE = 100

MOD_ROWS = 8
MOD_CTX_ROW = BATCH
VMEM_LIMIT = 56 * 1024 * 1024

NA_BLK_ROWS = 4
NA_BLK_KROWS = 12
NA_BLK_Q = NA_BLK_ROWS * GRID_W
NA_BLK_KEYS = NA_BLK_KROWS * GRID_W
NA_PATTERNS = 3
NA_UNROLL = 2


def _params(sem):
    return pltpu.CompilerParams(dimension_semantics=sem, vmem_limit_bytes=VMEM_LIMIT)


def _silu(v):
    return v * jax.nn.sigmoid(v)


def _mod_kernel(c_ref, w_ref, b_ref, o_ref):
    a = _silu(c_ref[...]).astype(BF16)
    o_ref[...] = jnp.dot(a, w_ref[...].astype(BF16), preferred_element_type=F32) + b_ref[...]


def _ada_table(c8, ada_w, ada_b3, l):
    tn = 512
    n = 3 * D_MODEL
    return pl.pallas_call(
        _mod_kernel,
        grid=(n // tn,),
        in_specs=[
            pl.BlockSpec((MOD_ROWS, D_MODEL), lambda j: (0, 0)),
            pl.BlockSpec((None, D_MODEL, tn), lambda j: (l, 0, j)),
            pl.BlockSpec((None, 1, tn), lambda j: (l, 0, j)),
        ],
        out_specs=pl.BlockSpec((MOD_ROWS, tn), lambda j: (0, j)),
        out_shape=jax.ShapeDtypeStruct((MOD_ROWS, n), F32),
        compiler_params=_params(("arbitrary",)),
        name="ada_table",
    )(c8, ada_w, ada_b3)


def _normmod_kernel(x_ref, g_ref, shift_ref, scale_ref, o_ref, *, tiles_per_row, fixed_row):
    if fixed_row is None:
        r = pl.program_id(0) // tiles_per_row
    else:
        r = fixed_row
    x = x_ref[...]
    y = x * lax.rsqrt(jnp.mean(x * x, axis=-1, keepdims=True) + NORM_EPS)
    y = y * g_ref[...]
    shift = shift_ref[pl.ds(r, 1), :]
    scale = scale_ref[pl.ds(r, 1), :]
    o_ref[...] = (y * (1.0 + scale) + shift).astype(o_ref.dtype)


def _norm_modulate(x2, norm_g3, l, mod, rows_per_mod, fixed_row):
    m = x2.shape[0]
    tm = 256
    tiles_per_row = None if rows_per_mod is None else rows_per_mod // tm
    kern = functools.partial(_normmod_kernel, tiles_per_row=tiles_per_row, fixed_row=fixed_row)
    return pl.pallas_call(
        kern,
        grid=(m // tm,),
        in_specs=[
            pl.BlockSpec((tm, D_MODEL), lambda i: (i, 0)),
            pl.BlockSpec((None, 1, D_MODEL), lambda i: (l, 0, 0)),
            pl.BlockSpec((MOD_ROWS, D_MODEL), lambda i: (0, 0)),
            pl.BlockSpec((MOD_ROWS, D_MODEL), lambda i: (0, 1)),
        ],
        out_specs=pl.BlockSpec((tm, D_MODEL), lambda i: (i, 0)),
        out_shape=jax.ShapeDtypeStruct((m, D_MODEL), BF16),
        compiler_params=_params(("arbitrary",)),
        name="norm_modulate",
    )(x2, norm_g3, mod, mod)


PROJ_TM = 1024
PROJ_TN = 512


def _mm_kernel(a_ref, w_ref, o_ref, wbf_ref):
    @pl.when(pl.program_id(1) == 0)
    def _():
        wbf_ref[...] = w_ref[...].astype(BF16)

    o_ref[...] = jnp.dot(a_ref[...], wbf_ref[...], preferred_element_type=F32)


def _ctx_kv_col_tile(j):
    b_kv = COL_B_K * HEAD_DIM // PROJ_TN
    c_kv = COL_C_K * HEAD_DIM // PROJ_TN
    d_kv = COL_D_K * HEAD_DIM // PROJ_TN
    n_c = 2 * GROUP_W // PROJ_TN
    return jnp.where(j < 1, b_kv, jnp.where(j < 1 + n_c, c_kv + (j - 1), d_kv + (j - 1 - n_c)))


CTX_KV_TILES = 1 + 2 * (2 * GROUP_W // PROJ_TN)
CTX_COLS_FULL = dict(b_k=COL_B_K, b_v=COL_B_V, c_k=COL_C_K, c_v=COL_C_V, d_k=COL_D_K, d_v=COL_D_V)
CTX_COLS_KV_ONLY = dict(b_k=0, b_v=KV_HEADS, c_k=2 * KV_HEADS, c_v=2 * KV_HEADS + N_HEADS,
                        d_k=2 * KV_HEADS + 2 * N_HEADS, d_v=2 * KV_HEADS + 3 * N_HEADS)


def _in_proj(h, w_in, l, kv_only=False):
    m = h.shape[0]
    tm, tn = PROJ_TM, PROJ_TN
    n_tiles = CTX_KV_TILES if kv_only else P_TOTAL // tn
    col = _ctx_kv_col_tile if kv_only else (lambda j: j)
    return pl.pallas_call(
        _mm_kernel,
        grid=(n_tiles, m // tm),
        in_specs=[
            pl.BlockSpec((tm, D_MODEL), lambda j, i: (i, 0)),
            pl.BlockSpec((None, D_MODEL, tn), lambda j, i: (l, 0, col(j))),
        ],
        out_specs=pl.BlockSpec((tm, tn), lambda j, i: (i, j)),
        out_shape=jax.ShapeDtypeStruct((m, n_tiles * tn), F32),
        scratch_shapes=[pltpu.VMEM((D_MODEL, tn), BF16)],
        compiler_params=_params(("arbitrary", "arbitrary")),
        name="in_proj",
    )(h, w_in)


def _out_kernel(ya_ref, yb_ref, yc_ref, yd_ref, w_ref, x_ref, gate_ref, o_ref, wbf_ref, *,
                tiles_per_row, fixed_row):
    if fixed_row is None:
        r = pl.program_id(1) // tiles_per_row
    else:
        r = fixed_row

    @pl.when(pl.program_id(1) == 0)
    def _():
        wbf_ref[...] = w_ref[...].astype(BF16)

    acc = jnp.dot(ya_ref[...], wbf_ref[0 * GROUP_W:1 * GROUP_W, :], preferred_element_type=F32)
    acc += jnp.dot(yb_ref[...], wbf_ref[1 * GROUP_W:2 * GROUP_W, :], preferred_element_type=F32)
    acc += jnp.dot(yc_ref[...], wbf_ref[2 * GROUP_W:3 * GROUP_W, :], preferred_element_type=F32)
    acc += jnp.dot(yd_ref[...], wbf_ref[3 * GROUP_W:4 * GROUP_W, :], preferred_element_type=F32)
    gate = gate_ref[pl.ds(r, 1), :]
    o_ref[...] = x_ref[...] + gate * acc


def _out_proj(ys, w_out, l, x2, mod, rows_per_mod, fixed_row):
    m = x2.shape[0]
    tm, tn = PROJ_TM, PROJ_TN
    tiles_per_row = None if rows_per_mod is None else rows_per_mod // tm
    kern = functools.partial(_out_kernel, tiles_per_row=tiles_per_row, fixed_row=fixed_row)
    y_spec = pl.BlockSpec((tm, GROUP_W), lambda j, i: (i, 0))
    gate_col0 = 2 * D_MODEL // tn
    return pl.pallas_call(
        kern,
        grid=(D_MODEL // tn, m // tm),
        in_specs=[
            y_spec, y_spec, y_spec, y_spec,
            pl.BlockSpec((None, D_MODEL, tn), lambda j, i: (l, 0, j)),
            pl.BlockSpec((tm, tn), lambda j, i: (i, j)),
            pl.BlockSpec((MOD_ROWS, tn), lambda j, i: (0, gate_col0 + j)),
        ],
        out_specs=pl.BlockSpec((tm, tn), lambda j, i: (i, j)),
        out_shape=jax.ShapeDtypeStruct((m, D_MODEL), F32),
        scratch_shapes=[pltpu.VMEM((D_MODEL, tn), BF16)],
        compiler_params=_params(("arbitrary", "arbitrary")),
        name="out_proj",
    )(*ys, w_out, x2, mod)


def _final_norm_kernel(x_ref, g_ref, o_ref):
    x = x_ref[...]
    o_ref[...] = x * lax.rsqrt(jnp.mean(x * x, axis=-1, keepdims=True) + NORM_EPS) * g_ref[...]


def _final_norm(x2, g2):
    m = x2.shape[0]
    tm = 256
    return pl.pallas_call(
        _final_norm_kernel,
        grid=(m // tm,),
        in_specs=[pl.BlockSpec((tm, D_MODEL), lambda i: (i, 0)),
                  pl.BlockSpec((1, D_MODEL), lambda i: (0, 0))],
        out_specs=pl.BlockSpec((tm, D_MODEL), lambda i: (i, 0)),
        out_shape=jax.ShapeDtypeStruct((m, D_MODEL), F32),
        compiler_params=_params(("arbitrary",)),
        name="final_norm",
    )(x2, g2)


CONV_HALO = 16
CONV_SUB = 64
SUBLANES = 8


def _conv_kernel(cur_ref, prev_ref, next_ref, gate_ref, w_ref, b_ref, lng_ref, lnb_ref, o_ref,
                 u_ref, ush_ref, c_ref, *, tile, n_tiles):
    t = pl.program_id(1)

    def glu(ref):
        return ref[:, :GROUP_W] * jax.nn.sigmoid(ref[:, GROUP_W:])

    u_ref[CONV_HALO:CONV_HALO + tile, :] = glu(cur_ref)
    u_ref[0:CONV_HALO, :] = jnp.where(t > 0, glu(prev_ref), 0.0)
    u_ref[CONV_HALO + tile:, :] = jnp.where(t < n_tiles - 1, glu(next_ref), 0.0)

    sh_rows = ush_ref.shape[1]
    for b in range(1, SUBLANES):
        ush_ref[b - 1] = u_ref[b:b + sh_rows, :]

    base = CONV_HALO - CONV_HALF

    def sub_tile(s, carry):
        row0 = pl.multiple_of(s * CONV_SUB, CONV_SUB)
        for c in range(GROUP_W // HEAD_DIM):
            lanes = slice(c * HEAD_DIM, (c + 1) * HEAD_DIM)
            acc = jnp.zeros((CONV_SUB, HEAD_DIM), F32)
            for k in range(CONV_WIDTH):
                b = (base + k) % SUBLANES
                rows = pl.ds(pl.multiple_of(row0 + (base + k - b), SUBLANES), CONV_SUB)
                tap = u_ref[rows, lanes] if b == 0 else ush_ref[b - 1, rows, lanes]
                acc = acc + w_ref[k:k + 1, lanes] * tap
            c_ref[pl.ds(row0, CONV_SUB), lanes] = acc + b_ref[:, lanes]
        return carry

    lax.fori_loop(0, tile // CONV_SUB, sub_tile, 0)

    v = c_ref[...]
    mu = jnp.mean(v, axis=-1, keepdims=True)
    var = jnp.mean(jnp.square(v - mu), axis=-1, keepdims=True)
    y = (v - mu) * lax.rsqrt(var + LN_EPS) * lng_ref[...] + lnb_ref[...]
    o_ref[...] = (_silu(y) * _silu(gate_ref[...])).astype(o_ref.dtype)


def _conv_mixer(p, seq_len, conv_w, conv_b3, ln_g3, ln_b3, l):
    tile = 256
    n_tiles = seq_len // tile
    n_seq = p.shape[0] // seq_len
    halo_per_tile = tile // CONV_HALO
    last_halo = p.shape[0] // CONV_HALO - 1
    kern = functools.partial(_conv_kernel, tile=tile, n_tiles=n_tiles)
    vec = pl.BlockSpec((None, 1, GROUP_W), lambda b, t: (l, 0, 0))
    return pl.pallas_call(
        kern,
        grid=(n_seq, n_tiles),
        in_specs=[
            pl.BlockSpec((tile, 2 * GROUP_W), lambda b, t: (b * n_tiles + t, 0)),
            pl.BlockSpec((CONV_HALO, 2 * GROUP_W),
                         lambda b, t: (jnp.maximum((b * n_tiles + t) * halo_per_tile - 1, 0), 0)),
            pl.BlockSpec((CONV_HALO, 2 * GROUP_W),
                         lambda b, t: (jnp.minimum((b * n_tiles + t + 1) * halo_per_tile, last_halo), 0)),
            pl.BlockSpec((tile, GROUP_W), lambda b, t: (b * n_tiles + t, COL_A_GATE // 8)),
            pl.BlockSpec((None, CONV_WIDTH, GROUP_W), lambda b, t: (l, 0, 0)),
            vec, vec, vec,
        ],
        out_specs=pl.BlockSpec((tile, GROUP_W), lambda b, t: (b * n_tiles + t, 0)),
        out_shape=jax.ShapeDtypeStruct((p.shape[0], GROUP_W), BF16),
        scratch_shapes=[pltpu.VMEM((tile + 2 * CONV_HALO, GROUP_W), F32),
                        pltpu.VMEM((SUBLANES - 1, tile + 2 * CONV_HALO - SUBLANES, GROUP_W), F32),
                        pltpu.VMEM((tile, GROUP_W), F32)],
        compiler_params=_params(("arbitrary", "arbitrary")),
        name="conv_mixer",
    )(p, p, p, p, conv_w, conv_b3, ln_g3, ln_b3)


def _head_rmsnorm(v, g):
    return v * lax.rsqrt(jnp.mean(v * v, axis=-1, keepdims=True) + NORM_EPS) * g


def _rope(v, cos, sin_a, sin_b):
    return (v * cos + pltpu.roll(v, HEAD_DIM - HEAD_DIM // 4, 1) * sin_a
            + pltpu.roll(v, HEAD_DIM // 4, 1) * sin_b)


def _dot_nt(a, b):
    return lax.dot_general(a, b, (((1,), (1,)), ((), ())), preferred_element_type=F32)


def _softmax_pv(s_list, v_list):
    m = s_list[0].max(axis=-1, keepdims=True)
    for s in s_list[1:]:
        m = jnp.maximum(m, s.max(axis=-1, keepdims=True))
    den = None
    acc = None
    for s, v in zip(s_list, v_list):
        e = jnp.exp(s - m)
        d = e.sum(axis=-1, keepdims=True)
        o = jnp.dot(e.astype(BF16), v, preferred_element_type=F32)
        den = d if den is None else den + d
        acc = o if acc is None else acc + o
    return acc / den


def _gqa_kernel(*refs, tq, with_ctx):
    if with_ctx:
        (q_ref, kl_ref, vl_ref, kc_ref, vc_ref, g_ref, qc_ref, gc_ref,
         cos_q, sa_q, sb_q, cos_k, sa_k, sb_k, qn_ref, kn_ref,
         o_ref, oc_ref, k_sc, v_sc) = refs
    else:
        (q_ref, kl_ref, vl_ref, kc_ref, vc_ref, g_ref,
         cos_q, sa_q, sb_q, cos_k, sa_k, sb_k, qn_ref, kn_ref,
         o_ref, k_sc, v_sc) = refs
    qi = pl.program_id(2)
    qn = qn_ref[...] * ATTN_SCALE
    kn = kn_ref[...]

    @pl.when(qi == 0)
    def _():
        k_sc[0:CTX_LEN, :] = _head_rmsnorm(kc_ref[...], kn).astype(BF16)
        k_sc[CTX_LEN:, :] = _rope(_head_rmsnorm(kl_ref[...], kn),
                                  cos_k[...], sa_k[...], sb_k[...]).astype(BF16)
        v_sc[0:CTX_LEN, :] = vc_ref[...].astype(BF16)
        v_sc[CTX_LEN:, :] = vl_ref[...].astype(BF16)
        if with_ctx:
            kc = k_sc[0:CTX_LEN, :]
            vc = v_sc[0:CTX_LEN, :]
            for h in range(Q_PER_KV):
                lanes = slice(h * HEAD_DIM, (h + 1) * HEAD_DIM)
                qh = _head_rmsnorm(qc_ref[:, lanes], qn).astype(BF16)
                o = _softmax_pv([_dot_nt(qh, kc)], [vc])
                oc_ref[:, lanes] = (o * _silu(gc_ref[:, lanes])).astype(oc_ref.dtype)

    k_all = k_sc[...]
    v_all = v_sc[...]
    for h in range(Q_PER_KV):
        lanes = slice(h * HEAD_DIM, (h + 1) * HEAD_DIM)
        qh = _rope(_head_rmsnorm(q_ref[:, lanes], qn), cos_q[...], sa_q[...], sb_q[...]).astype(BF16)
        o = _softmax_pv([_dot_nt(qh, k_all)], [v_all])
        o_ref[:, lanes] = (o * _silu(g_ref[:, lanes])).astype(o_ref.dtype)


def _gqa_mixer(p_lat, p_ctx, ctx_cols, rope_tabs, qn3, kn3, l, with_ctx):
    tq = 256
    ck, cv = ctx_cols["b_k"], ctx_cols["b_v"]
    nq = SEQ // tq
    qw = Q_PER_KV * HEAD_DIM
    cos, sin_a, sin_b = rope_tabs
    kern = functools.partial(_gqa_kernel, tq=tq, with_ctx=with_ctx)
    in_specs = [
        pl.BlockSpec((tq, qw), lambda b, kv, qi: (b * nq + qi, COL_B_Q // Q_PER_KV + kv)),
        pl.BlockSpec((SEQ, HEAD_DIM), lambda b, kv, qi: (b, COL_B_K + kv)),
        pl.BlockSpec((SEQ, HEAD_DIM), lambda b, kv, qi: (b, COL_B_V + kv)),
        pl.BlockSpec((CTX_LEN, HEAD_DIM), lambda b, kv, qi: (b, ck + kv)),
        pl.BlockSpec((CTX_LEN, HEAD_DIM), lambda b, kv, qi: (b, cv + kv)),
        pl.BlockSpec((tq, qw), lambda b, kv, qi: (b * nq + qi, COL_B_GATE // Q_PER_KV + kv)),
    ]
    args = [p_lat, p_lat, p_lat, p_ctx, p_ctx, p_lat]
    if with_ctx:
        in_specs += [
            pl.BlockSpec((CTX_LEN, qw), lambda b, kv, qi: (b, COL_B_Q // Q_PER_KV + kv)),
            pl.BlockSpec((CTX_LEN, qw), lambda b, kv, qi: (b, COL_B_GATE // Q_PER_KV + kv)),
        ]
        args += [p_ctx, p_ctx]
    tab_q = pl.BlockSpec((tq, HEAD_DIM), lambda b, kv, qi: (qi, 0))
    tab_k = pl.BlockSpec((SEQ, HEAD_DIM), lambda b, kv, qi: (0, 0))
    gvec = pl.BlockSpec((None, 1, HEAD_DIM), lambda b, kv, qi: (l, 0, 0))
    in_specs += [tab_q, tab_q, tab_q, tab_k, tab_k, tab_k, gvec, gvec]
    args += [cos, sin_a, sin_b, cos, sin_a, sin_b, qn3, kn3]
    out_specs = [pl.BlockSpec((tq, qw), lambda b, kv, qi: (b * nq + qi, kv))]
    out_shape = [jax.ShapeDtypeStruct((BATCH * SEQ, GROUP_W), BF16)]
    if with_ctx:
        out_specs.append(pl.BlockSpec((CTX_LEN, qw), lambda b, kv, qi: (b, kv)))
        out_shape.append(jax.ShapeDtypeStruct((BATCH * CTX_LEN, GROUP_W), BF16))
    res = pl.pallas_call(
        kern,
        grid=(BATCH, KV_HEADS, nq),
        in_specs=in_specs,
        out_specs=out_specs,
        out_shape=out_shape,
        scratch_shapes=[pltpu.VMEM((CTX_LEN + SEQ, HEAD_DIM), BF16),
                        pltpu.VMEM((CTX_LEN + SEQ, HEAD_DIM), BF16)],
        compiler_params=_params(("arbitrary", "arbitrary", "arbitrary")),
        name="gqa_mixer",
    )(*args)
    return (res[0], res[1]) if with_ctx else (res[0], None)


def _log_sigmoid(v):
    return -(jnp.maximum(-v, 0.0) + jnp.log1p(jnp.exp(-jnp.abs(v))))


RET_HEADS = 2
RET_UNROLL = 2


def _ret_kernel(*refs, with_ctx):
    if with_ctx:
        (ql_ref, kl_ref, vl_ref, gl_ref, qc_ref, kc_ref, vc_ref, gc_ref,
         cos_ref, sa_ref, sb_ref, decf_ref, decb_ref, gng_ref, gnb_ref,
         o_ref, oc_ref, q_sc, k_sc, v_sc, accf_sc, accb_sc, cst_sc) = refs
    else:
        (ql_ref, kl_ref, vl_ref, gl_ref, kc_ref, vc_ref,
         cos_ref, sa_ref, sb_ref, decf_ref, decb_ref, gng_ref, gnb_ref,
         o_ref, q_sc, k_sc, v_sc, accf_sc, accb_sc, cst_sc) = refs
        qc_ref = None

    n_ctx = CTX_LEN // CHUNK
    n_lat = SEQ // CHUNK
    n_all = n_ctx + n_lat
    ii = lax.broadcasted_iota(jnp.int32, (CHUNK, CHUNK), 0).astype(F32)
    jj = lax.broadcasted_iota(jnp.int32, (CHUNK, CHUNK), 1).astype(F32)
    dfw = ii - jj
    cdec = []
    for hh in range(RET_HEADS):
        lanes = slice(hh * HEAD_DIM, (hh + 1) * HEAD_DIM)
        if with_ctx:
            q_sc[hh, 0:CTX_LEN, :] = qc_ref[:, lanes]
        q_sc[hh, CTX_LEN:, :] = _rope(ql_ref[:, lanes], cos_ref[...], sa_ref[...], sb_ref[...])
        k_sc[hh, 0:CTX_LEN, :] = kc_ref[:, lanes] * ATTN_SCALE
        k_sc[hh, CTX_LEN:, :] = _rope(kl_ref[:, lanes], cos_ref[...], sa_ref[...],
                                      sb_ref[...]) * ATTN_SCALE
        v_sc[hh, 0:CTX_LEN, :] = vc_ref[:, lanes].astype(BF16)
        v_sc[hh, CTX_LEN:, :] = vl_ref[:, lanes].astype(BF16)

        lg_f = _log_sigmoid(decf_ref[hh])
        lg_b = _log_sigmoid(decb_ref[hh])
        base = 6 * hh
        cst_sc[base + 0] = jnp.where(dfw >= 0, jnp.exp(jnp.maximum(dfw, 0.0) * lg_f), 0.0)
        cst_sc[base + 1] = jnp.exp((ii + 1.0) * lg_f)
        cst_sc[base + 2] = jnp.exp((CHUNK - 1.0 - ii) * lg_f)
        cst_sc[base + 3] = jnp.where(dfw <= 0, jnp.exp(jnp.maximum(-dfw, 0.0) * lg_b), 0.0)
        cst_sc[base + 4] = jnp.exp((CHUNK - ii) * lg_b)
        cst_sc[base + 5] = jnp.exp(ii * lg_b)
        cdec += [jnp.exp(CHUNK * lg_f), jnp.exp(CHUNK * lg_b)]

    def chunk_step(hh, direction, c, s, acc_ref, need_out):
        base = 6 * hh + 3 * direction
        rows = pl.ds(pl.multiple_of(c * CHUNK, CHUNK), CHUNK)
        kc = k_sc[hh, rows, :]
        vc = v_sc[hh, rows, :]
        if need_out:
            qc = q_sc[hh, rows, :]
            att = _dot_nt(qc.astype(BF16), kc.astype(BF16)) * cst_sc[base]
            o = jnp.dot(att.astype(BF16), vc, preferred_element_type=F32)
            o = o + jnp.dot((qc * cst_sc[base + 1]).astype(BF16), s.astype(BF16),
                            preferred_element_type=F32)
            acc_ref[hh, rows, :] = o
        kd_t = jnp.transpose(kc * cst_sc[base + 2]).astype(BF16)
        return s * cdec[2 * hh + direction] + jnp.dot(kd_t, vc, preferred_element_type=F32)

    states = []
    for hh in range(RET_HEADS):
        sf = jnp.zeros((HEAD_DIM, HEAD_DIM), F32)
        sb = jnp.zeros((HEAD_DIM, HEAD_DIM), F32)
        for c in range(n_ctx):
            sf = chunk_step(hh, 0, c, sf, accf_sc, with_ctx)
            sb = chunk_step(hh, 1, n_ctx - 1 - c, sb, accb_sc, with_ctx)
        states += [sf, sb]

    def lat_step(t, carry):
        new = []
        for hh in range(RET_HEADS):
            new.append(chunk_step(hh, 0, n_ctx + t, carry[2 * hh], accf_sc, True))
            new.append(chunk_step(hh, 1, n_all - 1 - t, carry[2 * hh + 1], accb_sc, True))
        return tuple(new)

    lax.fori_loop(0, n_lat, lat_step, tuple(states), unroll=RET_UNROLL)

    def group_norm(o, gate, lanes):
        mu = jnp.mean(o, axis=-1, keepdims=True)
        var = jnp.mean(jnp.square(o - mu), axis=-1, keepdims=True)
        on = (o - mu) * lax.rsqrt(var + LN_EPS)
        return (on * gng_ref[:, lanes] + gnb_ref[:, lanes]) * _silu(gate)

    for hh in range(RET_HEADS):
        lanes = slice(hh * HEAD_DIM, (hh + 1) * HEAD_DIM)
        o = accf_sc[hh, CTX_LEN:, :] + accb_sc[hh, CTX_LEN:, :]
        o_ref[:, lanes] = group_norm(o, gl_ref[:, lanes], lanes).astype(o_ref.dtype)
        if with_ctx:
            oc = accf_sc[hh, 0:CTX_LEN, :] + accb_sc[hh, 0:CTX_LEN, :]
            oc_ref[:, lanes] = group_norm(oc, gc_ref[:, lanes], lanes).astype(oc_ref.dtype)


def _retention_mixer(p_lat, p_ctx, ctx_cols, rope_tabs, dec_f, dec_b, gn_g3, gn_b3, l, with_ctx):
    cos, sin_a, sin_b = rope_tabs
    kern = functools.partial(_ret_kernel, with_ctx=with_ctx)
    width = RET_HEADS * HEAD_DIM

    def col(c0, rows):
        return pl.BlockSpec((rows, width), lambda b, h: (b, c0 // RET_HEADS + h))

    in_specs = [col(COL_C_Q, SEQ), col(COL_C_K, SEQ), col(COL_C_V, SEQ), col(COL_C_GATE, SEQ)]
    args = [p_lat, p_lat, p_lat, p_lat]
    if with_ctx:
        in_specs += [col(COL_C_Q, CTX_LEN), col(COL_C_K, CTX_LEN), col(COL_C_V, CTX_LEN),
                     col(COL_C_GATE, CTX_LEN)]
        args += [p_ctx, p_ctx, p_ctx, p_ctx]
    else:
        in_specs += [col(ctx_cols["c_k"], CTX_LEN), col(ctx_cols["c_v"], CTX_LEN)]
        args += [p_ctx, p_ctx]
    tab = pl.BlockSpec((SEQ, HEAD_DIM), lambda b, h: (0, 0))
    dec = pl.BlockSpec((None, RET_HEADS, 1, HEAD_DIM), lambda b, h: (l, h, 0, 0))
    gvec = pl.BlockSpec((None, 1, width), lambda b, h: (l, 0, h))
    in_specs += [tab, tab, tab, dec, dec, gvec, gvec]
    args += [cos, sin_a, sin_b, dec_f, dec_b, gn_g3, gn_b3]
    out_specs = [pl.BlockSpec((SEQ, width), lambda b, h: (b, h))]
    out_shape = [jax.ShapeDtypeStruct((BATCH * SEQ, GROUP_W), BF16)]
    if with_ctx:
        out_specs.append(pl.BlockSpec((CTX_LEN, width), lambda b, h: (b, h)))
        out_shape.append(jax.ShapeDtypeStruct((BATCH * CTX_LEN, GROUP_W), BF16))
    n_rows = CTX_LEN + SEQ
    res = pl.pallas_call(
        kern,
        grid=(BATCH, N_HEADS // RET_HEADS),
        in_specs=in_specs,
        out_specs=out_specs,
        out_shape=out_shape,
        scratch_shapes=[pltpu.VMEM((RET_HEADS, n_rows, HEAD_DIM), F32),
                        pltpu.VMEM((RET_HEADS, n_rows, HEAD_DIM), F32),
                        pltpu.VMEM((RET_HEADS, n_rows, HEAD_DIM), BF16),
                        pltpu.VMEM((RET_HEADS, n_rows, HEAD_DIM), F32),
                        pltpu.VMEM((RET_HEADS, n_rows, HEAD_DIM), F32),
                        pltpu.VMEM((RET_HEADS * 6, CHUNK, CHUNK), F32)],
        compiler_params=_params(("arbitrary", "arbitrary")),
        name="retention_mixer",
    )(*args)
    return (res[0], res[1]) if with_ctx else (res[0], None)


def _na_row_start(r):
    return min(max(r - NA_WIN_R // 2, 0), GRID_H - NA_WIN_R)


def _na_key_start(blk):
    return min(max(blk * NA_BLK_ROWS - NA_WIN_R // 2, 0), GRID_H - NA_BLK_KROWS)


def _na_table_kernel(tab_ref, bias_ref, mask_ref):
    q = lax.broadcasted_iota(jnp.int32, (GRID_W, HEAD_DIM), 0)
    lane = lax.broadcasted_iota(jnp.int32, (GRID_W, HEAD_DIM), 1)
    kcol = lane % GRID_W
    lower_half = lane < GRID_W
    off_c = jnp.clip(kcol - q, -(NA_WIN_C - 1), NA_WIN_C - 1) + (NA_WIN_C - 1)
    c_start = jnp.clip(q - NA_WIN_C // 2, 0, GRID_W - NA_WIN_C)
    col_win = jnp.where((kcol >= c_start) & (kcol < c_start + NA_WIN_C), 1.0, 0.0)
    zeros = jnp.zeros((GRID_W, HEAD_DIM), F32)
    planes = []
    for o in range(2 * NA_WIN_R - 1):
        g = zeros
        for c in range(2 * NA_WIN_C - 1):
            g = jnp.where(off_c == c, tab_ref[0, o, c], g)
        planes.append(g)
    n_blk = GRID_H // NA_BLK_ROWS
    for pat, blk in enumerate((0, 1, n_blk - 1)):
        k0 = _na_key_start(blk)
        for rq in range(NA_BLK_ROWS):
            r = blk * NA_BLK_ROWS + rq
            rs = _na_row_start(r)
            for jp in range(NA_BLK_KROWS // 2):
                b_half, m_half = [], []
                for key_row in (k0 + 2 * jp, k0 + 2 * jp + 1):
                    valid = rs <= key_row < rs + NA_WIN_R
                    b_half.append(planes[key_row - r + NA_WIN_R - 1] if valid else zeros)
                    m_half.append(col_win if valid else zeros)
                rows = slice(rq * GRID_W, (rq + 1) * GRID_W)
                lanes = slice(jp * HEAD_DIM, (jp + 1) * HEAD_DIM)
                bias_ref[0, pat, rows, lanes] = jnp.where(lower_half, b_half[0], b_half[1])
                mask_ref[0, pat, rows, lanes] = jnp.where(lower_half, m_half[0], m_half[1])


def _na_tables(na_bias_l):
    spec = pl.BlockSpec((1, NA_PATTERNS, NA_BLK_Q, NA_BLK_KEYS), lambda h: (h, 0, 0, 0))
    shape = jax.ShapeDtypeStruct((N_HEADS, NA_PATTERNS, NA_BLK_Q, NA_BLK_KEYS), F32)
    return pl.pallas_call(
        _na_table_kernel,
        grid=(N_HEADS,),
        in_specs=[pl.BlockSpec((1, 2 * NA_WIN_R - 1, 2 * NA_WIN_C - 1), lambda h: (h, 0, 0),
                               memory_space=pltpu.SMEM)],
        out_specs=[spec, spec],
        out_shape=[shape, shape],
        compiler_params=_params(("arbitrary",)),
        name="na_tables",
    )(na_bias_l)


def _na_kernel(*refs, with_ctx):
    if with_ctx:
        (q_ref, k_ref, v_ref, g_ref, qc_ref, kc_ref, vc_ref, gc_ref, bias_ref, mask_ref,
         o_ref, oc_ref, q_sc, k_sc, v_sc, scx_sc) = refs
    else:
        (q_ref, k_ref, v_ref, g_ref, kc_ref, vc_ref, bias_ref, mask_ref,
         o_ref, q_sc, k_sc, v_sc, scx_sc) = refs
    q_sc[...] = q_ref[...].astype(BF16)
    k_sc[...] = k_ref[...].astype(BF16)
    v_sc[...] = v_ref[...].astype(BF16)
    kc = kc_ref[...].astype(BF16)
    vc = vc_ref[...].astype(BF16)
    scx_sc[...] = _dot_nt(q_sc[...], kc) * ATTN_SCALE
    n_blk = GRID_H // NA_BLK_ROWS

    def blk_step(blk, carry):
        k0 = jnp.clip(blk * NA_BLK_ROWS - NA_WIN_R // 2, 0, GRID_H - NA_BLK_KROWS)
        pat = jnp.where(blk == 0, 0, jnp.where(blk == n_blk - 1, 2, 1))
        rows = pl.ds(pl.multiple_of(blk * NA_BLK_Q, NA_BLK_Q), NA_BLK_Q)
        keys = pl.ds(pl.multiple_of(k0 * GRID_W, GRID_W), NA_BLK_KEYS)
        s_nb = _dot_nt(q_sc[rows, :], k_sc[keys, :]) * ATTN_SCALE + bias_ref[0, pat]
        s_nb = jnp.where(mask_ref[0, pat] > 0.5, s_nb, MASK_VALUE)
        o = _softmax_pv([s_nb, scx_sc[rows, :]], [v_sc[keys, :], vc])
        o_ref[rows, :] = (o * _silu(g_ref[rows, :])).astype(o_ref.dtype)
        return carry

    lax.fori_loop(0, n_blk, blk_step, 0, unroll=NA_UNROLL)

    if with_ctx:
        s = _dot_nt(qc_ref[...].astype(BF16), kc) * ATTN_SCALE
        o = _softmax_pv([s], [vc])
        oc_ref[...] = (o * _silu(gc_ref[...])).astype(oc_ref.dtype)


def _na_mixer(p_lat, p_ctx, ctx_cols, na_tabs, with_ctx):
    kern = functools.partial(_na_kernel, with_ctx=with_ctx)

    def col(c0, rows):
        return pl.BlockSpec((rows, HEAD_DIM), lambda b, h: (b, c0 + h))

    in_specs = [col(COL_D_Q, SEQ), col(COL_D_K, SEQ), col(COL_D_V, SEQ), col(COL_D_GATE, SEQ)]
    args = [p_lat, p_lat, p_lat, p_lat]
    if with_ctx:
        in_specs += [col(COL_D_Q, CTX_LEN), col(COL_D_K, CTX_LEN), col(COL_D_V, CTX_LEN),
                     col(COL_D_GATE, CTX_LEN)]
        args += [p_ctx, p_ctx, p_ctx, p_ctx]
    else:
        in_specs += [col(ctx_cols["d_k"], CTX_LEN), col(ctx_cols["d_v"], CTX_LEN)]
        args += [p_ctx, p_ctx]
    tab = pl.BlockSpec((1, NA_PATTERNS, NA_BLK_Q, NA_BLK_KEYS), lambda b, h: (h, 0, 0, 0))
    in_specs += [tab, tab]
    args += list(na_tabs)
    out_specs = [pl.BlockSpec((SEQ, HEAD_DIM), lambda b, h: (b, h))]
    out_shape = [jax.ShapeDtypeStruct((BATCH * SEQ, GROUP_W), BF16)]
    if with_ctx:
        out_specs.append(pl.BlockSpec((CTX_LEN, HEAD_DIM), lambda b, h: (b, h)))
        out_shape.append(jax.ShapeDtypeStruct((BATCH * CTX_LEN, GROUP_W), BF16))
    res = pl.pallas_call(
        kern,
        grid=(BATCH, N_HEADS),
        in_specs=in_specs,
        out_specs=out_specs,
        out_shape=out_shape,
        scratch_shapes=[pltpu.VMEM((SEQ, HEAD_DIM), BF16), pltpu.VMEM((SEQ, HEAD_DIM), BF16),
                        pltpu.VMEM((SEQ, HEAD_DIM), BF16), pltpu.VMEM((SEQ, CTX_LEN), F32)],
        compiler_params=_params(("arbitrary", "arbitrary")),
        name="na_mixer",
    )(*args)
    return (res[0], res[1]) if with_ctx else (res[0], None)


def _rope_tables():
    t = jnp.arange(SEQ)
    row = (t // GRID_W).astype(F32)
    col = (t % GRID_W).astype(F32)
    half = HEAD_DIM // 2
    inv_freq = ROPE_THETA ** (-jnp.arange(0, half, 2, dtype=F32) / half)
    ang_r = row[:, None] * inv_freq[None, :]
    ang_c = col[:, None] * inv_freq[None, :]
    ang = jnp.concatenate([ang_r, ang_r, ang_c, ang_c], axis=-1)
    cos, sin = jnp.cos(ang), jnp.sin(ang)
    quarter = (jnp.arange(HEAD_DIM) // (HEAD_DIM // 4)) % 2
    sin_a = jnp.where(quarter[None, :] == 0, -sin, 0.0)
    sin_b = jnp.where(quarter[None, :] == 1, sin, 0.0)
    return cos, sin_a, sin_b


def kernel(x, c, ctx, c_ctx, ada_w, ada_b, norm_g, w_in, conv_w, conv_b, conv_ln_g, conv_ln_b,
           gqa_qn_g, gqa_kn_g, ret_decay_fwd, ret_decay_bwd, ret_gn_g, ret_gn_b, na_bias,
           w_out, final_g):
    rope_tabs = _rope_tables()
    x_lat = x.reshape(BATCH * SEQ, D_MODEL)
    x_ctx = ctx.reshape(BATCH * CTX_LEN, D_MODEL)
    c8 = jnp.concatenate(
        [c, c_ctx[None, :], jnp.zeros((MOD_ROWS - BATCH - 1, D_MODEL), F32)], axis=0)

    def vec3(a):
        return a.reshape(DEPTH, 1, a.shape[-1])

    ada_b3, norm_g3 = vec3(ada_b), vec3(norm_g)
    conv_b3, ln_g3, ln_b3 = vec3(conv_b), vec3(conv_ln_g), vec3(conv_ln_b)
    qn3, kn3 = vec3(gqa_qn_g), vec3(gqa_kn_g)
    gn_g3, gn_b3 = vec3(ret_gn_g), vec3(ret_gn_b)
    dec_f = jnp.broadcast_to(ret_decay_fwd[:, :, None, None], (DEPTH, N_HEADS, 1, HEAD_DIM))
    dec_b = jnp.broadcast_to(ret_decay_bwd[:, :, None, None], (DEPTH, N_HEADS, 1, HEAD_DIM))

    for l in range(DEPTH):
        with_ctx = l < DEPTH - 1
        mod = _ada_table(c8, ada_w, ada_b3, l)
        h_lat = _norm_modulate(x_lat, norm_g3, l, mod, SEQ, None)
        h_ctx = _norm_modulate(x_ctx, norm_g3, l, mod, None, MOD_CTX_ROW)
        p_lat = _in_proj(h_lat, w_in, l)
        p_ctx = _in_proj(h_ctx, w_in, l, kv_only=not with_ctx)
        ctx_cols = CTX_COLS_FULL if with_ctx else CTX_COLS_KV_ONLY

        ya_l = _conv_mixer(p_lat, SEQ, conv_w, conv_b3, ln_g3, ln_b3, l)
        yb_l, yb_c = _gqa_mixer(p_lat, p_ctx, ctx_cols, rope_tabs, qn3, kn3, l, with_ctx)
        yc_l, yc_c = _retention_mixer(p_lat, p_ctx, ctx_cols, rope_tabs, dec_f, dec_b, gn_g3, gn_b3,
                                      l, with_ctx)
        yd_l, yd_c = _na_mixer(p_lat, p_ctx, ctx_cols, _na_tables(na_bias[l]), with_ctx)

        x_new = _out_proj((ya_l, yb_l, yc_l, yd_l), w_out, l, x_lat, mod, SEQ, None)
        if with_ctx:
            ya_c = _conv_mixer(p_ctx, CTX_LEN, conv_w, conv_b3, ln_g3, ln_b3, l)
            x_ctx = _out_proj((ya_c, yb_c, yc_c, yd_c), w_out, l, x_ctx, mod, None, MOD_CTX_ROW)
        x_lat = x_new

    out = _final_norm(x_lat, final_g.reshape(1, D_MODEL))
    return out.reshape(BATCH, SEQ, D_MODEL)
```

```python
import functools

import jax
import jax.numpy as jnp
from jax import lax
from jax.experimental import pallas as pl
from jax.experimental.pallas import tpu as pltpu

F32 = jnp.float32
BF16 = jnp.bfloat16

D_MODEL = 4096
BATCH = 4
SEQ = 2048
DEPTH = 2
GRID_W = 64
GRID_H = SEQ // GRID_W
CTX_LEN = 256
GROUP_W = 1024
HEAD_DIM = 128
N_HEADS = GROUP_W // HEAD_DIM
KV_HEADS = 2
Q_PER_KV = N_HEADS // KV_HEADS
CONV_WIDTH = 31
CONV_HALF = CONV_WIDTH // 2
CHUNK = 128
NA_WIN_R = 8
NA_WIN_C = 16
ROPE_THETA = 10000.0
NORM_EPS = 1e-6
LN_EPS = 1e-5
ATTN_SCALE = HEAD_DIM ** -0.5
MASK_VALUE = -1e30
P_TOTAL = 13 * GROUP_W + 2 * KV_HEADS * HEAD_DIM

COL_A_GLU = 0
COL_A_GATE = 16
COL_B_Q = 24
COL_B_K = 32
COL_B_V = 34
COL_B_GATE = 36
COL_C_Q = 44
COL_C_K = 52
COL_C_V = 60
COL_C_GATE = 68
COL_D_Q = 76
COL_D_K = 84
COL_D_V = 92
COL_D_GATE = 100

MOD_ROWS = 8
MOD_CTX_ROW = BATCH
VMEM_LIMIT = 56 * 1024 * 1024

NA_BLK_ROWS = 4
NA_BLK_KROWS = 12
NA_BLK_Q = NA_BLK_ROWS * GRID_W
NA_BLK_KEYS = NA_BLK_KROWS * GRID_W
NA_PATTERNS = 3
NA_UNROLL = 2


def _params(sem, flags=None):
    return pltpu.CompilerParams(dimension_semantics=sem, vmem_limit_bytes=VMEM_LIMIT, flags=flags)


def _silu(v):
    return v * jax.nn.sigmoid(v)


def _mod_kernel(c_ref, w_ref, b_ref, o_ref):
    a = _silu(c_ref[...]).astype(BF16)
    o_ref[...] = jnp.dot(a, w_ref[...].astype(BF16), preferred_element_type=F32) + b_ref[...]


def _ada_table(c8, ada_w, ada_b3, l):
    tn = 512
    n = 3 * D_MODEL
    return pl.pallas_call(
        _mod_kernel,
        grid=(n // tn,),
        in_specs=[
            pl.BlockSpec((MOD_ROWS, D_MODEL), lambda j: (0, 0)),
            pl.BlockSpec((None, D_MODEL, tn), lambda j: (l, 0, j)),
            pl.BlockSpec((None, 1, tn), lambda j: (l, 0, j)),
        ],
        out_specs=pl.BlockSpec((MOD_ROWS, tn), lambda j: (0, j)),
        out_shape=jax.ShapeDtypeStruct((MOD_ROWS, n), F32),
        compiler_params=_params(("arbitrary",)),
        name="ada_table",
    )(c8, ada_w, ada_b3)


def _normmod_kernel(x_ref, g_ref, shift_ref, scale_ref, o_ref, *, tiles_per_row, fixed_row):
    if fixed_row is None:
        r = pl.program_id(0) // tiles_per_row
    else:
        r = fixed_row
    x = x_ref[...]
    y = x * lax.rsqrt(jnp.mean(x * x, axis=-1, keepdims=True) + NORM_EPS)
    y = y * g_ref[...]
    shift = shift_ref[pl.ds(r, 1), :]
    scale = scale_ref[pl.ds(r, 1), :]
    o_ref[...] = (y * (1.0 + scale) + shift).astype(o_ref.dtype)


def _norm_modulate(x2, norm_g3, l, mod, rows_per_mod, fixed_row):
    m = x2.shape[0]
    tm = 256
    tiles_per_row = None if rows_per_mod is None else rows_per_mod // tm
    kern = functools.partial(_normmod_kernel, tiles_per_row=tiles_per_row, fixed_row=fixed_row)
    return pl.pallas_call(
        kern,
        grid=(m // tm,),
        in_specs=[
            pl.BlockSpec((tm, D_MODEL), lambda i: (i, 0)),
            pl.BlockSpec((None, 1, D_MODEL), lambda i: (l, 0, 0)),
            pl.BlockSpec((MOD_ROWS, D_MODEL), lambda i: (0, 0)),
            pl.BlockSpec((MOD_ROWS, D_MODEL), lambda i: (0, 1)),
        ],
        out_specs=pl.BlockSpec((tm, D_MODEL), lambda i: (i, 0)),
        out_shape=jax.ShapeDtypeStruct((m, D_MODEL), BF16),
        compiler_params=_params(("arbitrary",)),
        name="norm_modulate",
    )(x2, norm_g3, mod, mod)


PROJ_TM = 1024
PROJ_TN = 512


def _stage_weight_chunk(w_ref, wbf_ref, n_cols):
    j, i = pl.program_id(0), pl.program_id(1)
    chunk = w_ref.shape[0]

    @pl.when(j < n_cols)
    def _():
        wbf_ref[j % 2, pl.ds(pl.multiple_of(i * chunk, chunk), chunk), :] = w_ref[...].astype(BF16)

    return j, (j + 1) % 2


def _mm_kernel(a_ref, w_ref, o_ref, wbf_ref, *, n_cols):
    j, slot = _stage_weight_chunk(w_ref, wbf_ref, n_cols)

    @pl.when(j == 0)
    def _():
        o_ref[...] = jnp.zeros_like(o_ref)

    @pl.when(j > 0)
    def _():
        o_ref[...] = jnp.dot(a_ref[...], wbf_ref[slot], preferred_element_type=F32)


def _ctx_kv_col_tile(j):
    b_kv = COL_B_K * HEAD_DIM // PROJ_TN
    c_kv = COL_C_K * HEAD_DIM // PROJ_TN
    d_kv = COL_D_K * HEAD_DIM // PROJ_TN
    n_c = 2 * GROUP_W // PROJ_TN
    return jnp.where(j < 1, b_kv, jnp.where(j < 1 + n_c, c_kv + (j - 1), d_kv + (j - 1 - n_c)))


CTX_KV_TILES = 1 + 2 * (2 * GROUP_W // PROJ_TN)
CTX_COLS_FULL = dict(b_k=COL_B_K, b_v=COL_B_V, c_k=COL_C_K, c_v=COL_C_V, d_k=COL_D_K, d_v=COL_D_V)
CTX_COLS_KV_ONLY = dict(b_k=0, b_v=KV_HEADS, c_k=2 * KV_HEADS, c_v=2 * KV_HEADS + N_HEADS,
                        d_k=2 * KV_HEADS + 2 * N_HEADS, d_v=2 * KV_HEADS + 3 * N_HEADS)


def _in_proj(h, w_in, l, kv_only=False):
    m = h.shape[0]
    tm, tn = PROJ_TM, PROJ_TN
    n_tiles = CTX_KV_TILES if kv_only else P_TOTAL // tn
    col = _ctx_kv_col_tile if kv_only else (lambda j: j)
    n_rows = m // tm
    chunk = D_MODEL // n_rows
    return pl.pallas_call(
        functools.partial(_mm_kernel, n_cols=n_tiles),
        grid=(n_tiles + 1, n_rows),
        in_specs=[
            pl.BlockSpec((tm, D_MODEL), lambda j, i: (jnp.where(j == 0, 0, i), 0)),
            pl.BlockSpec((None, chunk, tn), lambda j, i: (l, i, col(jnp.minimum(j, n_tiles - 1)))),
        ],
        out_specs=pl.BlockSpec((tm, tn), lambda j, i: (i, jnp.maximum(j - 1, 0))),
        out_shape=jax.ShapeDtypeStruct((m, n_tiles * tn), F32),
        scratch_shapes=[pltpu.VMEM((2, D_MODEL, tn), BF16)],
        compiler_params=_params(("arbitrary", "arbitrary")),
        name="in_proj",
    )(h, w_in)


def _out_kernel(ya_ref, yb_ref, yc_ref, yd_ref, w_ref, x_ref, gate_ref, o_ref, wbf_ref, *,
                tiles_per_row, fixed_row, n_cols):
    if fixed_row is None:
        r = pl.program_id(1) // tiles_per_row
    else:
        r = fixed_row
    j, slot = _stage_weight_chunk(w_ref, wbf_ref, n_cols)

    @pl.when(j == 0)
    def _():
        o_ref[...] = jnp.zeros_like(o_ref)

    @pl.when(j > 0)
    def _():
        acc = None
        for g, y_ref in enumerate((ya_ref, yb_ref, yc_ref, yd_ref)):
            part = jnp.dot(y_ref[...], wbf_ref[slot, g * GROUP_W:(g + 1) * GROUP_W, :],
                           preferred_element_type=F32)
            acc = part if acc is None else acc + part
        gate = gate_ref[pl.ds(r, 1), :]
        o_ref[...] = x_ref[...] + gate * acc


def _out_proj(ys, w_out, l, x2, mod, rows_per_mod, fixed_row):
    m = x2.shape[0]
    tm, tn = PROJ_TM, PROJ_TN
    tiles_per_row = None if rows_per_mod is None else rows_per_mod // tm
    n_cols = D_MODEL // tn
    n_rows = m // tm
    chunk = D_MODEL // n_rows
    kern = functools.partial(_out_kernel, tiles_per_row=tiles_per_row, fixed_row=fixed_row,
                             n_cols=n_cols)

    def out_col(j):
        return jnp.maximum(j - 1, 0)

    y_spec = pl.BlockSpec((tm, GROUP_W), lambda j, i: (jnp.where(j == 0, 0, i), 0))
    gate_col0 = 2 * D_MODEL // tn
    return pl.pallas_call(
        kern,
        grid=(n_cols + 1, n_rows),
        in_specs=[
            y_spec, y_spec, y_spec, y_spec,
            pl.BlockSpec((None, chunk, tn), lambda j, i: (l, i, jnp.minimum(j, n_cols - 1))),
            pl.BlockSpec((tm, tn), lambda j, i: (i, out_col(j))),
            pl.BlockSpec((MOD_ROWS, tn), lambda j, i: (0, gate_col0 + out_col(j))),
        ],
        out_specs=pl.BlockSpec((tm, tn), lambda j, i: (i, out_col(j))),
        out_shape=jax.ShapeDtypeStruct((m, D_MODEL), F32),
        scratch_shapes=[pltpu.VMEM((2, D_MODEL, tn), BF16)],
        compiler_params=_params(("arbitrary", "arbitrary")),
        name="out_proj",
    )(*ys, w_out, x2, mod)


def _final_norm_kernel(x_ref, g_ref, o_ref):
    x = x_ref[...]
    o_ref[...] = x * lax.rsqrt(jnp.mean(x * x, axis=-1, keepdims=True) + NORM_EPS) * g_ref[...]


def _final_norm(x2, g2):
    m = x2.shape[0]
    tm = 256
    return pl.pallas_call(
        _final_norm_kernel,
        grid=(m // tm,),
        in_specs=[pl.BlockSpec((tm, D_MODEL), lambda i: (i, 0)),
                  pl.BlockSpec((1, D_MODEL), lambda i: (0, 0))],
        out_specs=pl.BlockSpec((tm, D_MODEL), lambda i: (i, 0)),
        out_shape=jax.ShapeDtypeStruct((m, D_MODEL), F32),
        compiler_params=_params(("arbitrary",)),
        name="final_norm",
    )(x2, g2)


CONV_HALO = 16
CONV_SUB = 64
SUBLANES = 8


def _conv_kernel(cur_ref, prev_ref, next_ref, gate_ref, w_ref, b_ref, lng_ref, lnb_ref, o_ref,
                 u_ref, ush_ref, c_ref, *, tile, n_tiles):
    t = pl.program_id(1)

    def glu(ref):
        return ref[:, :GROUP_W] * jax.nn.sigmoid(ref[:, GROUP_W:])

    u_ref[CONV_HALO:CONV_HALO + tile, :] = glu(cur_ref)
    u_ref[0:CONV_HALO, :] = jnp.where(t > 0, glu(prev_ref), 0.0)
    u_ref[CONV_HALO + tile:, :] = jnp.where(t < n_tiles - 1, glu(next_ref), 0.0)

    sh_rows = ush_ref.shape[1]
    for b in range(1, SUBLANES):
        ush_ref[b - 1] = u_ref[b:b + sh_rows, :]

    base = CONV_HALO - CONV_HALF

    for s in range(tile // CONV_SUB):
        row0 = s * CONV_SUB
        for c in range(GROUP_W // HEAD_DIM):
            lanes = slice(c * HEAD_DIM, (c + 1) * HEAD_DIM)
            acc = jnp.zeros((CONV_SUB, HEAD_DIM), F32)
            for k in range(CONV_WIDTH):
                b = (base + k) % SUBLANES
                rows = slice(row0 + base + k - b, row0 + base + k - b + CONV_SUB)
                tap = u_ref[rows, lanes] if b == 0 else ush_ref[b - 1, rows, lanes]
                acc = acc + w_ref[k:k + 1, lanes] * tap
            c_ref[row0:row0 + CONV_SUB, lanes] = acc + b_ref[:, lanes]

    v = c_ref[...]
    mu = jnp.mean(v, axis=-1, keepdims=True)
    var = jnp.mean(jnp.square(v - mu), axis=-1, keepdims=True)
    y = (v - mu) * lax.rsqrt(var + LN_EPS) * lng_ref[...] + lnb_ref[...]
    o_ref[...] = (_silu(y) * _silu(gate_ref[...])).astype(o_ref.dtype)


def _conv_mixer(p, seq_len, conv_w, conv_b3, ln_g3, ln_b3, l):
    tile = 256
    n_tiles = seq_len // tile
    n_seq = p.shape[0] // seq_len
    halo_per_tile = tile // CONV_HALO
    last_halo = p.shape[0] // CONV_HALO - 1
    kern = functools.partial(_conv_kernel, tile=tile, n_tiles=n_tiles)
    vec = pl.BlockSpec((None, 1, GROUP_W), lambda b, t: (l, 0, 0))
    return pl.pallas_call(
        kern,
        grid=(n_seq, n_tiles),
        in_specs=[
            pl.BlockSpec((tile, 2 * GROUP_W), lambda b, t: (b * n_tiles + t, 0)),
            pl.BlockSpec((CONV_HALO, 2 * GROUP_W),
                         lambda b, t: (jnp.maximum((b * n_tiles + t) * halo_per_tile - 1, 0), 0)),
            pl.BlockSpec((CONV_HALO, 2 * GROUP_W),
                         lambda b, t: (jnp.minimum((b * n_tiles + t + 1) * halo_per_tile, last_halo), 0)),
            pl.BlockSpec((tile, GROUP_W), lambda b, t: (b * n_tiles + t, COL_A_GATE // 8)),
            pl.BlockSpec((None, CONV_WIDTH, GROUP_W), lambda b, t: (l, 0, 0)),
            vec, vec, vec,
        ],
        out_specs=pl.BlockSpec((tile, GROUP_W), lambda b, t: (b * n_tiles + t, 0)),
        out_shape=jax.ShapeDtypeStruct((p.shape[0], GROUP_W), BF16),
        scratch_shapes=[pltpu.VMEM((tile + 2 * CONV_HALO, GROUP_W), F32),
                        pltpu.VMEM((SUBLANES - 1, tile + 2 * CONV_HALO - SUBLANES, GROUP_W), F32),
                        pltpu.VMEM((tile, GROUP_W), F32)],
        compiler_params=_params(("arbitrary", "arbitrary")),
        name="conv_mixer",
    )(p, p, p, p, conv_w, conv_b3, ln_g3, ln_b3)


def _head_rmsnorm(v, g):
    return v * lax.rsqrt(jnp.mean(v * v, axis=-1, keepdims=True) + NORM_EPS) * g


def _rope(v, cos, sin_a, sin_b):
    return (v * cos + pltpu.roll(v, HEAD_DIM - HEAD_DIM // 4, 1) * sin_a
            + pltpu.roll(v, HEAD_DIM // 4, 1) * sin_b)


def _dot_nt(a, b):
    return lax.dot_general(a, b, (((1,), (1,)), ((), ())), preferred_element_type=F32)


LOG2E = 1.4426950408889634


def _softmax_pv(s_list, v_list):
    m = s_list[0].max(axis=-1, keepdims=True)
    for s in s_list[1:]:
        m = jnp.maximum(m, s.max(axis=-1, keepdims=True))
    den = None
    acc = None
    for s, v in zip(s_list, v_list):
        e = jnp.exp2(s - m)
        d = e.sum(axis=-1, keepdims=True)
        o = jnp.dot(e.astype(BF16), v, preferred_element_type=F32)
        den = d if den is None else den + d
        acc = o if acc is None else acc + o
    return acc / den


def _gqa_kernel(*refs, tq, with_ctx):
    if with_ctx:
        (q_ref, kl_ref, vl_ref, kc_ref, vc_ref, g_ref, qc_ref, gc_ref,
         cos_q, sa_q, sb_q, cos_k, sa_k, sb_k, qn_ref, kn_ref,
         o_ref, oc_ref, k_sc, v_sc) = refs
    else:
        (q_ref, kl_ref, vl_ref, kc_ref, vc_ref, g_ref,
         cos_q, sa_q, sb_q, cos_k, sa_k, sb_k, qn_ref, kn_ref,
         o_ref, k_sc, v_sc) = refs
    qi = pl.program_id(2)
    qn = qn_ref[...] * (ATTN_SCALE * LOG2E)
    kn = kn_ref[...]

    @pl.when(qi == 0)
    def _():
        k_sc[0:CTX_LEN, :] = _head_rmsnorm(kc_ref[...], kn).astype(BF16)
        k_sc[CTX_LEN:, :] = _rope(_head_rmsnorm(kl_ref[...], kn),
                                  cos_k[...], sa_k[...], sb_k[...]).astype(BF16)
        v_sc[0:CTX_LEN, :] = vc_ref[...].astype(BF16)
        v_sc[CTX_LEN:, :] = vl_ref[...].astype(BF16)
        if with_ctx:
            kc = k_sc[0:CTX_LEN, :]
            vc = v_sc[0:CTX_LEN, :]
            for h in range(Q_PER_KV):
                lanes = slice(h * HEAD_DIM, (h + 1) * HEAD_DIM)
                qh = _head_rmsnorm(qc_ref[:, lanes], qn).astype(BF16)
                o = _softmax_pv([_dot_nt(qh, kc)], [vc])
                oc_ref[:, lanes] = (o * _silu(gc_ref[:, lanes])).astype(oc_ref.dtype)

    k_all = k_sc[...]
    v_all = v_sc[...]
    for h in range(Q_PER_KV):
        lanes = slice(h * HEAD_DIM, (h + 1) * HEAD_DIM)
        for r in range(tq // GQA_ROWS):
            rows = slice(r * GQA_ROWS, (r + 1) * GQA_ROWS)
            qh = _rope(_head_rmsnorm(q_ref[rows, lanes], qn),
                       cos_q[rows, :], sa_q[rows, :], sb_q[rows, :]).astype(BF16)
            o = _softmax_pv([_dot_nt(qh, k_all)], [v_all])
            o_ref[rows, lanes] = (o * _silu(g_ref[rows, lanes])).astype(o_ref.dtype)


GQA_ROWS = 256
GQA_TQ = 512


def _gqa_mixer(p_lat, p_ctx, ctx_cols, rope_tabs, qn3, kn3, l, with_ctx):
    tq = GQA_TQ
    ck, cv = ctx_cols["b_k"], ctx_cols["b_v"]
    nq = SEQ // tq
    qw = Q_PER_KV * HEAD_DIM
    cos, sin_a, sin_b = rope_tabs
    kern = functools.partial(_gqa_kernel, tq=tq, with_ctx=with_ctx)
    in_specs = [
        pl.BlockSpec((tq, qw), lambda b, kv, qi: (b * nq + qi, COL_B_Q // Q_PER_KV + kv)),
        pl.BlockSpec((SEQ, HEAD_DIM), lambda b, kv, qi: (b, COL_B_K + kv)),
        pl.BlockSpec((SEQ, HEAD_DIM), lambda b, kv, qi: (b, COL_B_V + kv)),
        pl.BlockSpec((CTX_LEN, HEAD_DIM), lambda b, kv, qi: (b, ck + kv)),
        pl.BlockSpec((CTX_LEN, HEAD_DIM), lambda b, kv, qi: (b, cv + kv)),
        pl.BlockSpec((tq, qw), lambda b, kv, qi: (b * nq + qi, COL_B_GATE // Q_PER_KV + kv)),
    ]
    args = [p_lat, p_lat, p_lat, p_ctx, p_ctx, p_lat]
    if with_ctx:
        in_specs += [
            pl.BlockSpec((CTX_LEN, qw), lambda b, kv, qi: (b, COL_B_Q // Q_PER_KV + kv)),
            pl.BlockSpec((CTX_LEN, qw), lambda b, kv, qi: (b, COL_B_GATE // Q_PER_KV + kv)),
        ]
        args += [p_ctx, p_ctx]
    tab_q = pl.BlockSpec((tq, HEAD_DIM), lambda b, kv, qi: (qi, 0))
    tab_k = pl.BlockSpec((SEQ, HEAD_DIM), lambda b, kv, qi: (0, 0))
    gvec = pl.BlockSpec((None, 1, HEAD_DIM), lambda b, kv, qi: (l, 0, 0))
    in_specs += [tab_q, tab_q, tab_q, tab_k, tab_k, tab_k, gvec, gvec]
    args += [cos, sin_a, sin_b, cos, sin_a, sin_b, qn3, kn3]
    out_specs = [pl.BlockSpec((tq, qw), lambda b, kv, qi: (b * nq + qi, kv))]
    out_shape = [jax.ShapeDtypeStruct((BATCH * SEQ, GROUP_W), BF16)]
    if with_ctx:
        out_specs.append(pl.BlockSpec((CTX_LEN, qw), lambda b, kv, qi: (b, kv)))
        out_shape.append(jax.ShapeDtypeStruct((BATCH * CTX_LEN, GROUP_W), BF16))
    res = pl.pallas_call(
        kern,
        grid=(BATCH, KV_HEADS, nq),
        in_specs=in_specs,
        out_specs=out_specs,
        out_shape=out_shape,
        scratch_shapes=[pltpu.VMEM((CTX_LEN + SEQ, HEAD_DIM), BF16),
                        pltpu.VMEM((CTX_LEN + SEQ, HEAD_DIM), BF16)],
        compiler_params=_params(("arbitrary", "arbitrary", "arbitrary")),
        name="gqa_mixer",
    )(*args)
    return (res[0], res[1]) if with_ctx else (res[0], None)


def _log_sigmoid(v):
    return -(jnp.maximum(-v, 0.0) + jnp.log1p(jnp.exp(-jnp.abs(v))))


RET_HEADS = 2
RET_UNROLL = 4


def _ret_kernel(*refs, with_ctx):
    if with_ctx:
        (ql_ref, kl_ref, vl_ref, gl_ref, qc_ref, kc_ref, vc_ref, gc_ref,
         cos_ref, sa_ref, sb_ref, decf_ref, decb_ref, gng_ref, gnb_ref,
         o_ref, oc_ref, q_sc, k_sc, v_sc, accf_sc, accb_sc, cst_sc) = refs
    else:
        (ql_ref, kl_ref, vl_ref, gl_ref, kc_ref, vc_ref,
         cos_ref, sa_ref, sb_ref, decf_ref, decb_ref, gng_ref, gnb_ref,
         o_ref, q_sc, k_sc, v_sc, accf_sc, accb_sc, cst_sc) = refs
        qc_ref = None

    n_ctx = CTX_LEN // CHUNK
    n_lat = SEQ // CHUNK
    n_all = n_ctx + n_lat
    ii = lax.broadcasted_iota(jnp.int32, (CHUNK, CHUNK), 0).astype(F32)
    jj = lax.broadcasted_iota(jnp.int32, (CHUNK, CHUNK), 1).astype(F32)
    dfw = ii - jj
    cdec = []
    for hh in range(RET_HEADS):
        lanes = slice(hh * HEAD_DIM, (hh + 1) * HEAD_DIM)
        if with_ctx:
            q_sc[hh, 0:CTX_LEN, :] = qc_ref[:, lanes]
        q_sc[hh, CTX_LEN:, :] = _rope(ql_ref[:, lanes], cos_ref[...], sa_ref[...], sb_ref[...])
        k_sc[hh, 0:CTX_LEN, :] = kc_ref[:, lanes] * ATTN_SCALE
        k_sc[hh, CTX_LEN:, :] = _rope(kl_ref[:, lanes], cos_ref[...], sa_ref[...],
                                      sb_ref[...]) * ATTN_SCALE
        v_sc[hh, 0:CTX_LEN, :] = vc_ref[:, lanes].astype(BF16)
        v_sc[hh, CTX_LEN:, :] = vl_ref[:, lanes].astype(BF16)

        lg_f = _log_sigmoid(decf_ref[hh])
        lg_b = _log_sigmoid(decb_ref[hh])
        base = 6 * hh
        cst_sc[base + 0] = jnp.where(dfw >= 0, jnp.exp(jnp.maximum(dfw, 0.0) * lg_f), 0.0)
        cst_sc[base + 1] = jnp.exp((ii + 1.0) * lg_f)
        cst_sc[base + 2] = jnp.exp((CHUNK - 1.0 - ii) * lg_f)
        cst_sc[base + 3] = jnp.where(dfw <= 0, jnp.exp(jnp.maximum(-dfw, 0.0) * lg_b), 0.0)
        cst_sc[base + 4] = jnp.exp((CHUNK - ii) * lg_b)
        cst_sc[base + 5] = jnp.exp(ii * lg_b)
        cdec += [jnp.exp(CHUNK * lg_f), jnp.exp(CHUNK * lg_b)]

    def chunk_step(hh, direction, c, s, acc_ref, need_out):
        base = 6 * hh + 3 * direction
        rows = pl.ds(pl.multiple_of(c * CHUNK, CHUNK), CHUNK)
        kc = k_sc[hh, rows, :]
        vc = v_sc[hh, rows, :]
        if need_out:
            qc = q_sc[hh, rows, :]
            att = _dot_nt(qc.astype(BF16), kc.astype(BF16)) * cst_sc[base]
            o = jnp.dot(att.astype(BF16), vc, preferred_element_type=F32)
            o = o + jnp.dot((qc * cst_sc[base + 1]).astype(BF16), s.astype(BF16),
                            preferred_element_type=F32)
            acc_ref[hh, rows, :] = o
        kd_t = jnp.transpose(kc * cst_sc[base + 2]).astype(BF16)
        return s * cdec[2 * hh + direction] + jnp.dot(kd_t, vc, preferred_element_type=F32)

    states = []
    for hh in range(RET_HEADS):
        sf = jnp.zeros((HEAD_DIM, HEAD_DIM), F32)
        sb = jnp.zeros((HEAD_DIM, HEAD_DIM), F32)
        for c in range(n_ctx):
            sf = chunk_step(hh, 0, c, sf, accf_sc, with_ctx)
            sb = chunk_step(hh, 1, n_ctx - 1 - c, sb, accb_sc, with_ctx)
        states += [sf, sb]

    def lat_step(t, carry):
        new = []
        for hh in range(RET_HEADS):
            new.append(chunk_step(hh, 0, n_ctx + t, carry[2 * hh], accf_sc, True))
            new.append(chunk_step(hh, 1, n_all - 1 - t, carry[2 * hh + 1], accb_sc, True))
        return tuple(new)

    lax.fori_loop(0, n_lat, lat_step, tuple(states), unroll=RET_UNROLL)

    def group_norm(o, gate, lanes):
        mu = jnp.mean(o, axis=-1, keepdims=True)
        var = jnp.mean(jnp.square(o - mu), axis=-1, keepdims=True)
        on = (o - mu) * lax.rsqrt(var + LN_EPS)
        return (on * gng_ref[:, lanes] + gnb_ref[:, lanes]) * _silu(gate)

    for hh in range(RET_HEADS):
        lanes = slice(hh * HEAD_DIM, (hh + 1) * HEAD_DIM)
        o = accf_sc[hh, CTX_LEN:, :] + accb_sc[hh, CTX_LEN:, :]
        o_ref[:, lanes] = group_norm(o, gl_ref[:, lanes], lanes).astype(o_ref.dtype)
        if with_ctx:
            oc = accf_sc[hh, 0:CTX_LEN, :] + accb_sc[hh, 0:CTX_LEN, :]
            oc_ref[:, lanes] = group_norm(oc, gc_ref[:, lanes], lanes).astype(oc_ref.dtype)


def _retention_mixer(p_lat, p_ctx, ctx_cols, rope_tabs, dec_f, dec_b, gn_g3, gn_b3, l, with_ctx):
    cos, sin_a, sin_b = rope_tabs
    kern = functools.partial(_ret_kernel, with_ctx=with_ctx)
    width = RET_HEADS * HEAD_DIM

    def col(c0, rows):
        return pl.BlockSpec((rows, width), lambda b, h: (b, c0 // RET_HEADS + h))

    in_specs = [col(COL_C_Q, SEQ), col(COL_C_K, SEQ), col(COL_C_V, SEQ), col(COL_C_GATE, SEQ)]
    args = [p_lat, p_lat, p_lat, p_lat]
    if with_ctx:
        in_specs += [col(COL_C_Q, CTX_LEN), col(COL_C_K, CTX_LEN), col(COL_C_V, CTX_LEN),
                     col(COL_C_GATE, CTX_LEN)]
        args += [p_ctx, p_ctx, p_ctx, p_ctx]
    else:
        in_specs += [col(ctx_cols["c_k"], CTX_LEN), col(ctx_cols["c_v"], CTX_LEN)]
        args += [p_ctx, p_ctx]
    tab = pl.BlockSpec((SEQ, HEAD_DIM), lambda b, h: (0, 0))
    dec = pl.BlockSpec((None, RET_HEADS, 1, HEAD_DIM), lambda b, h: (l, h, 0, 0))
    gvec = pl.BlockSpec((None, 1, width), lambda b, h: (l, 0, h))
    in_specs += [tab, tab, tab, dec, dec, gvec, gvec]
    args += [cos, sin_a, sin_b, dec_f, dec_b, gn_g3, gn_b3]
    out_specs = [pl.BlockSpec((SEQ, width), lambda b, h: (b, h))]
    out_shape = [jax.ShapeDtypeStruct((BATCH * SEQ, GROUP_W), BF16)]
    if with_ctx:
        out_specs.append(pl.BlockSpec((CTX_LEN, width), lambda b, h: (b, h)))
        out_shape.append(jax.ShapeDtypeStruct((BATCH * CTX_LEN, GROUP_W), BF16))
    n_rows = CTX_LEN + SEQ
    res = pl.pallas_call(
        kern,
        grid=(BATCH, N_HEADS // RET_HEADS),
        in_specs=in_specs,
        out_specs=out_specs,
        out_shape=out_shape,
        scratch_shapes=[pltpu.VMEM((RET_HEADS, n_rows, HEAD_DIM), F32),
                        pltpu.VMEM((RET_HEADS, n_rows, HEAD_DIM), F32),
                        pltpu.VMEM((RET_HEADS, n_rows, HEAD_DIM), BF16),
                        pltpu.VMEM((RET_HEADS, n_rows, HEAD_DIM), F32),
                        pltpu.VMEM((RET_HEADS, n_rows, HEAD_DIM), F32),
                        pltpu.VMEM((RET_HEADS * 6, CHUNK, CHUNK), F32)],
        compiler_params=_params(("arbitrary", "arbitrary")),
        name="retention_mixer",
    )(*args)
    return (res[0], res[1]) if with_ctx else (res[0], None)


def _na_row_start(r):
    return min(max(r - NA_WIN_R // 2, 0), GRID_H - NA_WIN_R)


def _na_key_start(blk):
    return min(max(blk * NA_BLK_ROWS - NA_WIN_R // 2, 0), GRID_H - NA_BLK_KROWS)


def _na_table_kernel(tab_ref, bias_ref, mask_ref):
    q = lax.broadcasted_iota(jnp.int32, (GRID_W, HEAD_DIM), 0)
    lane = lax.broadcasted_iota(jnp.int32, (GRID_W, HEAD_DIM), 1)
    kcol = lane % GRID_W
    lower_half = lane < GRID_W
    off_c = jnp.clip(kcol - q, -(NA_WIN_C - 1), NA_WIN_C - 1) + (NA_WIN_C - 1)
    c_start = jnp.clip(q - NA_WIN_C // 2, 0, GRID_W - NA_WIN_C)
    col_win = jnp.where((kcol >= c_start) & (kcol < c_start + NA_WIN_C), 1.0, 0.0)
    zeros = jnp.zeros((GRID_W, HEAD_DIM), F32)
    planes = []
    for o in range(2 * NA_WIN_R - 1):
        g = zeros
        for c in range(2 * NA_WIN_C - 1):
            g = jnp.where(off_c == c, tab_ref[0, o, c] * LOG2E, g)
        planes.append(g)
    n_blk = GRID_H // NA_BLK_ROWS
    for pat, blk in enumerate((0, 1, n_blk - 1)):
        k0 = _na_key_start(blk)
        for rq in range(NA_BLK_ROWS):
            r = blk * NA_BLK_ROWS + rq
            rs = _na_row_start(r)
            for jp in range(NA_BLK_KROWS // 2):
                b_half, m_half = [], []
                for key_row in (k0 + 2 * jp, k0 + 2 * jp + 1):
                    valid = rs <= key_row < rs + NA_WIN_R
                    b_half.append(planes[key_row - r + NA_WIN_R - 1] if valid else zeros)
                    m_half.append(col_win if valid else zeros)
                rows = slice(rq * GRID_W, (rq + 1) * GRID_W)
                lanes = slice(jp * HEAD_DIM, (jp + 1) * HEAD_DIM)
                bias_ref[0, pat, rows, lanes] = jnp.where(lower_half, b_half[0], b_half[1])
                mask_ref[0, pat, rows, lanes] = jnp.where(lower_half, m_half[0], m_half[1])


def _na_tables(na_bias_l):
    spec = pl.BlockSpec((1, NA_PATTERNS, NA_BLK_Q, NA_BLK_KEYS), lambda h: (h, 0, 0, 0))
    shape = jax.ShapeDtypeStruct((N_HEADS, NA_PATTERNS, NA_BLK_Q, NA_BLK_KEYS), F32)
    return pl.pallas_call(
        _na_table_kernel,
        grid=(N_HEADS,),
        in_specs=[pl.BlockSpec((1, 2 * NA_WIN_R - 1, 2 * NA_WIN_C - 1), lambda h: (h, 0, 0),
                               memory_space=pltpu.SMEM)],
        out_specs=[spec, spec],
        out_shape=[shape, shape],
        compiler_params=_params(("arbitrary",)),
        name="na_tables",
    )(na_bias_l)


def _na_kernel(*refs, with_ctx):
    if with_ctx:
        (q_ref, k_ref, v_ref, g_ref, qc_ref, kc_ref, vc_ref, gc_ref, bias_ref, mask_ref,
         o_ref, oc_ref, q_sc, k_sc, v_sc, scx_sc) = refs
    else:
        (q_ref, k_ref, v_ref, g_ref, kc_ref, vc_ref, bias_ref, mask_ref,
         o_ref, q_sc, k_sc, v_sc, scx_sc) = refs
    q_sc[...] = q_ref[...].astype(BF16)
    k_sc[...] = k_ref[...].astype(BF16)
    v_sc[...] = v_ref[...].astype(BF16)
    kc = kc_ref[...].astype(BF16)
    vc = vc_ref[...].astype(BF16)
    scale = ATTN_SCALE * LOG2E
    scx_sc[...] = _dot_nt(q_sc[...], kc) * scale
    n_blk = GRID_H // NA_BLK_ROWS

    def blk_step(blk, carry):
        k0 = jnp.clip(blk * NA_BLK_ROWS - NA_WIN_R // 2, 0, GRID_H - NA_BLK_KROWS)
        pat = jnp.where(blk == 0, 0, jnp.where(blk == n_blk - 1, 2, 1))
        rows = pl.ds(pl.multiple_of(blk * NA_BLK_Q, NA_BLK_Q), NA_BLK_Q)
        keys = pl.ds(pl.multiple_of(k0 * GRID_W, GRID_W), NA_BLK_KEYS)
        s_nb = _dot_nt(q_sc[rows, :], k_sc[keys, :]) * scale + bias_ref[0, pat]
        s_nb = jnp.where(mask_ref[0, pat] > 0.5, s_nb, MASK_VALUE)
        o = _softmax_pv([s_nb, scx_sc[rows, :]], [v_sc[keys, :], vc])
        o_ref[rows, :] = (o * _silu(g_ref[rows, :])).astype(o_ref.dtype)
        return carry

    lax.fori_loop(0, n_blk, blk_step, 0, unroll=NA_UNROLL)

    if with_ctx:
        s = _dot_nt(qc_ref[...].astype(BF16), kc) * scale
        o = _softmax_pv([s], [vc])
        oc_ref[...] = (o * _silu(gc_ref[...])).astype(oc_ref.dtype)


def _na_mixer(p_lat, p_ctx, ctx_cols, na_tabs, with_ctx):
    kern = functools.partial(_na_kernel, with_ctx=with_ctx)

    def col(c0, rows):
        return pl.BlockSpec((rows, HEAD_DIM), lambda b, h: (b, c0 + h))

    in_specs = [col(COL_D_Q, SEQ), col(COL_D_K, SEQ), col(COL_D_V, SEQ), col(COL_D_GATE, SEQ)]
    args = [p_lat, p_lat, p_lat, p_lat]
    if with_ctx:
        in_specs += [col(COL_D_Q, CTX_LEN), col(COL_D_K, CTX_LEN), col(COL_D_V, CTX_LEN),
                     col(COL_D_GATE, CTX_LEN)]
        args += [p_ctx, p_ctx, p_ctx, p_ctx]
    else:
        in_specs += [col(ctx_cols["d_k"], CTX_LEN), col(ctx_cols["d_v"], CTX_LEN)]
        args += [p_ctx, p_ctx]
    tab = pl.BlockSpec((1, NA_PATTERNS, NA_BLK_Q, NA_BLK_KEYS), lambda b, h: (h, 0, 0, 0))
    in_specs += [tab, tab]
    args += list(na_tabs)
    out_specs = [pl.BlockSpec((SEQ, HEAD_DIM), lambda b, h: (b, h))]
    out_shape = [jax.ShapeDtypeStruct((BATCH * SEQ, GROUP_W), BF16)]
    if with_ctx:
        out_specs.append(pl.BlockSpec((CTX_LEN, HEAD_DIM), lambda b, h: (b, h)))
        out_shape.append(jax.ShapeDtypeStruct((BATCH * CTX_LEN, GROUP_W), BF16))
    res = pl.pallas_call(
        kern,
        grid=(BATCH, N_HEADS),
        in_specs=in_specs,
        out_specs=out_specs,
        out_shape=out_shape,
        scratch_shapes=[pltpu.VMEM((SEQ, HEAD_DIM), BF16), pltpu.VMEM((SEQ, HEAD_DIM), BF16),
                        pltpu.VMEM((SEQ, HEAD_DIM), BF16), pltpu.VMEM((SEQ, CTX_LEN), F32)],
        compiler_params=_params(("arbitrary", "arbitrary")),
        name="na_mixer",
    )(*args)
    return (res[0], res[1]) if with_ctx else (res[0], None)


def _rope_tables():
    t = jnp.arange(SEQ)
    row = (t // GRID_W).astype(F32)
    col = (t % GRID_W).astype(F32)
    half = HEAD_DIM // 2
    inv_freq = ROPE_THETA ** (-jnp.arange(0, half, 2, dtype=F32) / half)
    ang_r = row[:, None] * inv_freq[None, :]
    ang_c = col[:, None] * inv_freq[None, :]
    ang = jnp.concatenate([ang_r, ang_r, ang_c, ang_c], axis=-1)
    cos, sin = jnp.cos(ang), jnp.sin(ang)
    quarter = (jnp.arange(HEAD_DIM) // (HEAD_DIM // 4)) % 2
    sin_a = jnp.where(quarter[None, :] == 0, -sin, 0.0)
    sin_b = jnp.where(quarter[None, :] == 1, sin, 0.0)
    return cos, sin_a, sin_b


def kernel(x, c, ctx, c_ctx, ada_w, ada_b, norm_g, w_in, conv_w, conv_b, conv_ln_g, conv_ln_b,
           gqa_qn_g, gqa_kn_g, ret_decay_fwd, ret_decay_bwd, ret_gn_g, ret_gn_b, na_bias,
           w_out, final_g):
    rope_tabs = _rope_tables()
    x_lat = x.reshape(BATCH * SEQ, D_MODEL)
    x_ctx = ctx.reshape(BATCH * CTX_LEN, D_MODEL)
    c8 = jnp.concatenate(
        [c, c_ctx[None, :], jnp.zeros((MOD_ROWS - BATCH - 1, D_MODEL), F32)], axis=0)

    def vec3(a):
        return a.reshape(DEPTH, 1, a.shape[-1])

    ada_b3, norm_g3 = vec3(ada_b), vec3(norm_g)
    conv_b3, ln_g3, ln_b3 = vec3(conv_b), vec3(conv_ln_g), vec3(conv_ln_b)
    qn3, kn3 = vec3(gqa_qn_g), vec3(gqa_kn_g)
    gn_g3, gn_b3 = vec3(ret_gn_g), vec3(ret_gn_b)
    dec_f = jnp.broadcast_to(ret_decay_fwd[:, :, None, None], (DEPTH, N_HEADS, 1, HEAD_DIM))
    dec_b = jnp.broadcast_to(ret_decay_bwd[:, :, None, None], (DEPTH, N_HEADS, 1, HEAD_DIM))

    for l in range(DEPTH):
        with_ctx = l < DEPTH - 1
        mod = _ada_table(c8, ada_w, ada_b3, l)
        h_lat = _norm_modulate(x_lat, norm_g3, l, mod, SEQ, None)
        h_ctx = _norm_modulate(x_ctx, norm_g3, l, mod, None, MOD_CTX_ROW)
        p_lat = _in_proj(h_lat, w_in, l)
        p_ctx = _in_proj(h_ctx, w_in, l, kv_only=not with_ctx)
        ctx_cols = CTX_COLS_FULL if with_ctx else CTX_COLS_KV_ONLY

        ya_l = _conv_mixer(p_lat, SEQ, conv_w, conv_b3, ln_g3, ln_b3, l)
        yb_l, yb_c = _gqa_mixer(p_lat, p_ctx, ctx_cols, rope_tabs, qn3, kn3, l, with_ctx)
        yc_l, yc_c = _retention_mixer(p_lat, p_ctx, ctx_cols, rope_tabs, dec_f, dec_b, gn_g3, gn_b3,
                                      l, with_ctx)
        yd_l, yd_c = _na_mixer(p_lat, p_ctx, ctx_cols, _na_tables(na_bias[l]), with_ctx)

        x_new = _out_proj((ya_l, yb_l, yc_l, yd_l), w_out, l, x_lat, mod, SEQ, None)
        if with_ctx:
            ya_c = _conv_mixer(p_ctx, CTX_LEN, conv_w, conv_b3, ln_g3, ln_b3, l)
            x_ctx = _out_proj((ya_c, yb_c, yc_c, yd_c), w_out, l, x_ctx, mod, None, MOD_CTX_ROW)
        x_lat = x_new

    out = _final_norm(x_lat, final_g.reshape(1, D_MODEL))
    return out.reshape(BATCH, SEQ, D_MODEL)
```

```python
import functools

import jax
import jax.numpy as jnp
from jax import lax
from jax.experimental import pallas as pl
from jax.experimental.pallas import tpu as pltpu

F32 = jnp.float32
BF16 = jnp.bfloat16

D_MODEL = 4096
BATCH = 4
SEQ = 2048
DEPTH = 2
GRID_W = 64
GRID_H = SEQ // GRID_W
CTX_LEN = 256
GROUP_W = 1024
HEAD_DIM = 128
N_HEADS = GROUP_W // HEAD_DIM
KV_HEADS = 2
Q_PER_KV = N_HEADS // KV_HEADS
CONV_WIDTH = 31
CONV_HALF = CONV_WIDTH // 2
CHUNK = 128
NA_WIN_R = 8
NA_WIN_C = 16
ROPE_THETA = 10000.0
NORM_EPS = 1e-6
LN_EPS = 1e-5
ATTN_SCALE = HEAD_DIM ** -0.5
MASK_VALUE = -1e30
P_TOTAL = 13 * GROUP_W + 2 * KV_HEADS * HEAD_DIM

COL_A_GLU = 0
COL_A_GATE = 16
COL_B_Q = 24
COL_B_K = 32
COL_B_V = 34
COL_B_GATE = 36
COL_C_Q = 44
COL_C_K = 52
COL_C_V = 60
COL_C_GATE = 68
COL_D_Q = 76
COL_D_K = 84
COL_D_V = 92
COL_D_GATE = 100

MOD_ROWS = 8
MOD_CTX_ROW = BATCH
VMEM_LIMIT = 56 * 1024 * 1024

NA_BLK_ROWS = 4
NA_BLK_KROWS = 12
NA_BLK_Q = NA_BLK_ROWS * GRID_W
NA_BLK_KEYS = NA_BLK_KROWS * GRID_W
NA_PATTERNS = 3
NA_UNROLL = 2


def _params(sem, flags=None):
    return pltpu.CompilerParams(dimension_semantics=sem, vmem_limit_bytes=VMEM_LIMIT, flags=flags)


def _silu(v):
    return v * jax.nn.sigmoid(v)


def _mod_kernel(c_ref, w_ref, b_ref, o_ref):
    a = _silu(c_ref[...]).astype(BF16)
    o_ref[...] = jnp.dot(a, w_ref[...].astype(BF16), preferred_element_type=F32) + b_ref[...]


def _ada_table(c8, ada_w, ada_b3, l):
    tn = 512
    n = 3 * D_MODEL
    return pl.pallas_call(
        _mod_kernel,
        grid=(n // tn,),
        in_specs=[
            pl.BlockSpec((MOD_ROWS, D_MODEL), lambda j: (0, 0)),
            pl.BlockSpec((None, D_MODEL, tn), lambda j: (l, 0, j)),
            pl.BlockSpec((None, 1, tn), lambda j: (l, 0, j)),
        ],
        out_specs=pl.BlockSpec((MOD_ROWS, tn), lambda j: (0, j)),
        out_shape=jax.ShapeDtypeStruct((MOD_ROWS, n), F32),
        compiler_params=_params(("arbitrary",)),
        name="ada_table",
    )(c8, ada_w, ada_b3)


def _normmod_kernel(x_ref, g_ref, shift_ref, scale_ref, o_ref, *, tiles_per_row, fixed_row):
    if fixed_row is None:
        r = pl.program_id(0) // tiles_per_row
    else:
        r = fixed_row
    x = x_ref[...]
    y = x * lax.rsqrt(jnp.mean(x * x, axis=-1, keepdims=True) + NORM_EPS)
    y = y * g_ref[...]
    shift = shift_ref[pl.ds(r, 1), :]
    scale = scale_ref[pl.ds(r, 1), :]
    o_ref[...] = (y * (1.0 + scale) + shift).astype(o_ref.dtype)


def _norm_modulate(x2, norm_g3, l, mod, rows_per_mod, fixed_row):
    m = x2.shape[0]
    tm = 256
    tiles_per_row = None if rows_per_mod is None else rows_per_mod // tm
    kern = functools.partial(_normmod_kernel, tiles_per_row=tiles_per_row, fixed_row=fixed_row)
    return pl.pallas_call(
        kern,
        grid=(m // tm,),
        in_specs=[
            pl.BlockSpec((tm, D_MODEL), lambda i: (i, 0)),
            pl.BlockSpec((None, 1, D_MODEL), lambda i: (l, 0, 0)),
            pl.BlockSpec((MOD_ROWS, D_MODEL), lambda i: (0, 0)),
            pl.BlockSpec((MOD_ROWS, D_MODEL), lambda i: (0, 1)),
        ],
        out_specs=pl.BlockSpec((tm, D_MODEL), lambda i: (i, 0)),
        out_shape=jax.ShapeDtypeStruct((m, D_MODEL), BF16),
        compiler_params=_params(("arbitrary",)),
        name="norm_modulate",
    )(x2, norm_g3, mod, mod)


PROJ_TM = 1024
PROJ_TN = 512


def _stage_weight_chunk(w_ref, wbf_ref, n_cols):
    j, i = pl.program_id(0), pl.program_id(1)
    chunk = w_ref.shape[0]

    @pl.when(j < n_cols)
    def _():
        wbf_ref[j % 2, pl.ds(pl.multiple_of(i * chunk, chunk), chunk), :] = w_ref[...].astype(BF16)

    return j, (j + 1) % 2


def _mm_kernel(a_ref, w_ref, o_ref, wbf_ref, *, n_cols):
    j, slot = _stage_weight_chunk(w_ref, wbf_ref, n_cols)

    @pl.when(j > 0)
    def _():
        o_ref[...] = jnp.dot(a_ref[...], wbf_ref[slot], preferred_element_type=F32)


def _ctx_kv_col_tile(j):
    b_kv = COL_B_K * HEAD_DIM // PROJ_TN
    c_kv = COL_C_K * HEAD_DIM // PROJ_TN
    d_kv = COL_D_K * HEAD_DIM // PROJ_TN
    n_c = 2 * GROUP_W // PROJ_TN
    return jnp.where(j < 1, b_kv, jnp.where(j < 1 + n_c, c_kv + (j - 1), d_kv + (j - 1 - n_c)))


CTX_KV_TILES = 1 + 2 * (2 * GROUP_W // PROJ_TN)
CTX_COLS_FULL = dict(b_k=COL_B_K, b_v=COL_B_V, c_k=COL_C_K, c_v=COL_C_V, d_k=COL_D_K, d_v=COL_D_V)
CTX_COLS_KV_ONLY = dict(b_k=0, b_v=KV_HEADS, c_k=2 * KV_HEADS, c_v=2 * KV_HEADS + N_HEADS,
                        d_k=2 * KV_HEADS + 2 * N_HEADS, d_v=2 * KV_HEADS + 3 * N_HEADS)


def _in_proj(h, w_in, l, kv_only=False):
    m = h.shape[0]
    tm, tn = PROJ_TM, PROJ_TN
    n_tiles = CTX_KV_TILES if kv_only else P_TOTAL // tn
    col = _ctx_kv_col_tile if kv_only else (lambda j: j)
    n_rows = m // tm
    chunk = D_MODEL // n_rows
    return pl.pallas_call(
        functools.partial(_mm_kernel, n_cols=n_tiles),
        grid=(n_tiles + 1, n_rows),
        in_specs=[
            pl.BlockSpec((tm, D_MODEL), lambda j, i: (jnp.where(j == 0, 0, i), 0)),
            pl.BlockSpec((None, chunk, tn), lambda j, i: (l, i, col(jnp.minimum(j, n_tiles - 1)))),
        ],
        out_specs=pl.BlockSpec((tm, tn), lambda j, i: (jnp.where(j == 0, 0, i), jnp.maximum(j - 1, 0))),
        out_shape=jax.ShapeDtypeStruct((m, n_tiles * tn), F32),
        scratch_shapes=[pltpu.VMEM((2, D_MODEL, tn), BF16)],
        compiler_params=_params(("arbitrary", "arbitrary")),
        name="in_proj",
    )(h, w_in)


def _out_kernel(ya_ref, yb_ref, yc_ref, yd_ref, w_ref, x_ref, gate_ref, o_ref, wbf_ref, *,
                tiles_per_row, fixed_row, n_cols):
    if fixed_row is None:
        r = pl.program_id(1) // tiles_per_row
    else:
        r = fixed_row
    j, slot = _stage_weight_chunk(w_ref, wbf_ref, n_cols)

    @pl.when(j > 0)
    def _():
        acc = None
        for g, y_ref in enumerate((ya_ref, yb_ref, yc_ref, yd_ref)):
            part = jnp.dot(y_ref[...], wbf_ref[slot, g * GROUP_W:(g + 1) * GROUP_W, :],
                           preferred_element_type=F32)
            acc = part if acc is None else acc + part
        gate = gate_ref[pl.ds(r, 1), :]
        o_ref[...] = x_ref[...] + gate * acc


def _out_proj(ys, w_out, l, x2, mod, rows_per_mod, fixed_row):
    m = x2.shape[0]
    tm, tn = PROJ_TM, PROJ_TN
    tiles_per_row = None if rows_per_mod is None else rows_per_mod // tm
    n_cols = D_MODEL // tn
    n_rows = m // tm
    chunk = D_MODEL // n_rows
    kern = functools.partial(_out_kernel, tiles_per_row=tiles_per_row, fixed_row=fixed_row,
                             n_cols=n_cols)

    def out_col(j):
        return jnp.maximum(j - 1, 0)

    def row(j, i):
        return jnp.where(j == 0, 0, i)

    y_spec = pl.BlockSpec((tm, GROUP_W), lambda j, i: (row(j, i), 0))
    gate_col0 = 2 * D_MODEL // tn
    return pl.pallas_call(
        kern,
        grid=(n_cols + 1, n_rows),
        in_specs=[
            y_spec, y_spec, y_spec, y_spec,
            pl.BlockSpec((None, chunk, tn), lambda j, i: (l, i, jnp.minimum(j, n_cols - 1))),
            pl.BlockSpec((tm, tn), lambda j, i: (row(j, i), out_col(j))),
            pl.BlockSpec((MOD_ROWS, tn), lambda j, i: (0, gate_col0 + out_col(j))),
        ],
        out_specs=pl.BlockSpec((tm, tn), lambda j, i: (row(j, i), out_col(j))),
        out_shape=jax.ShapeDtypeStruct((m, D_MODEL), F32),
        scratch_shapes=[pltpu.VMEM((2, D_MODEL, tn), BF16)],
        compiler_params=_params(("arbitrary", "arbitrary")),
        name="out_proj",
    )(*ys, w_out, x2, mod)


def _final_norm_kernel(x_ref, g_ref, o_ref):
    x = x_ref[...]
    o_ref[...] = x * lax.rsqrt(jnp.mean(x * x, axis=-1, keepdims=True) + NORM_EPS) * g_ref[...]


def _final_norm(x2, g2):
    m = x2.shape[0]
    tm = 256
    return pl.pallas_call(
        _final_norm_kernel,
        grid=(m // tm,),
        in_specs=[pl.BlockSpec((tm, D_MODEL), lambda i: (i, 0)),
                  pl.BlockSpec((1, D_MODEL), lambda i: (0, 0))],
        out_specs=pl.BlockSpec((tm, D_MODEL), lambda i: (i, 0)),
        out_shape=jax.ShapeDtypeStruct((m, D_MODEL), F32),
        compiler_params=_params(("arbitrary",)),
        name="final_norm",
    )(x2, g2)


CONV_HALO = 16
CONV_SUB = 64
SUBLANES = 8


def _conv_kernel(cur_ref, prev_ref, next_ref, gate_ref, w_ref, b_ref, lng_ref, lnb_ref, o_ref,
                 u_ref, ush_ref, c_ref, *, tile, n_tiles):
    t = pl.program_id(1)

    def glu(ref):
        return ref[:, :GROUP_W] * jax.nn.sigmoid(ref[:, GROUP_W:])

    u_ref[CONV_HALO:CONV_HALO + tile, :] = glu(cur_ref)
    u_ref[0:CONV_HALO, :] = jnp.where(t > 0, glu(prev_ref), 0.0)
    u_ref[CONV_HALO + tile:, :] = jnp.where(t < n_tiles - 1, glu(next_ref), 0.0)

    sh_rows = ush_ref.shape[1]
    for b in range(1, SUBLANES):
        ush_ref[b - 1] = u_ref[b:b + sh_rows, :]

    base = CONV_HALO - CONV_HALF

    for s in range(tile // CONV_SUB):
        row0 = s * CONV_SUB
        for c in range(GROUP_W // HEAD_DIM):
            lanes = slice(c * HEAD_DIM, (c + 1) * HEAD_DIM)
            acc = jnp.zeros((CONV_SUB, HEAD_DIM), F32)
            for k in range(CONV_WIDTH):
                b = (base + k) % SUBLANES
                rows = slice(row0 + base + k - b, row0 + base + k - b + CONV_SUB)
                tap = u_ref[rows, lanes] if b == 0 else ush_ref[b - 1, rows, lanes]
                acc = acc + w_ref[k:k + 1, lanes] * tap
            c_ref[row0:row0 + CONV_SUB, lanes] = acc + b_ref[:, lanes]

    v = c_ref[...]
    mu = jnp.mean(v, axis=-1, keepdims=True)
    var = jnp.mean(jnp.square(v - mu), axis=-1, keepdims=True)
    y = (v - mu) * lax.rsqrt(var + LN_EPS) * lng_ref[...] + lnb_ref[...]
    o_ref[...] = (_silu(y) * _silu(gate_ref[...])).astype(o_ref.dtype)


def _conv_mixer(p, seq_len, conv_w, conv_b3, ln_g3, ln_b3, l):
    tile = 256
    n_tiles = seq_len // tile
    n_seq = p.shape[0] // seq_len
    halo_per_tile = tile // CONV_HALO
    last_halo = p.shape[0] // CONV_HALO - 1
    kern = functools.partial(_conv_kernel, tile=tile, n_tiles=n_tiles)
    vec = pl.BlockSpec((None, 1, GROUP_W), lambda b, t: (l, 0, 0))
    return pl.pallas_call(
        kern,
        grid=(n_seq, n_tiles),
        in_specs=[
            pl.BlockSpec((tile, 2 * GROUP_W), lambda b, t: (b * n_tiles + t, 0)),
            pl.BlockSpec((CONV_HALO, 2 * GROUP_W),
                         lambda b, t: (jnp.maximum((b * n_tiles + t) * halo_per_tile - 1, 0), 0)),
            pl.BlockSpec((CONV_HALO, 2 * GROUP_W),
                         lambda b, t: (jnp.minimum((b * n_tiles + t + 1) * halo_per_tile, last_halo), 0)),
            pl.BlockSpec((tile, GROUP_W), lambda b, t: (b * n_tiles + t, COL_A_GATE // 8)),
            pl.BlockSpec((None, CONV_WIDTH, GROUP_W), lambda b, t: (l, 0, 0)),
            vec, vec, vec,
        ],
        out_specs=pl.BlockSpec((tile, GROUP_W), lambda b, t: (b * n_tiles + t, 0)),
        out_shape=jax.ShapeDtypeStruct((p.shape[0], GROUP_W), BF16),
        scratch_shapes=[pltpu.VMEM((tile + 2 * CONV_HALO, GROUP_W), F32),
                        pltpu.VMEM((SUBLANES - 1, tile + 2 * CONV_HALO - SUBLANES, GROUP_W), F32),
                        pltpu.VMEM((tile, GROUP_W), F32)],
        compiler_params=_params(("arbitrary", "arbitrary")),
        name="conv_mixer",
    )(p, p, p, p, conv_w, conv_b3, ln_g3, ln_b3)


def _head_rmsnorm(v, g):
    return v * lax.rsqrt(jnp.mean(v * v, axis=-1, keepdims=True) + NORM_EPS) * g


def _rope(v, cos, sin_a, sin_b):
    return (v * cos + pltpu.roll(v, HEAD_DIM - HEAD_DIM // 4, 1) * sin_a
            + pltpu.roll(v, HEAD_DIM // 4, 1) * sin_b)


def _dot_nt(a, b):
    return lax.dot_general(a, b, (((1,), (1,)), ((), ())), preferred_element_type=F32)


LOG2E = 1.4426950408889634


def _softmax_pv(s_list, v_list):
    m = s_list[0].max(axis=-1, keepdims=True)
    for s in s_list[1:]:
        m = jnp.maximum(m, s.max(axis=-1, keepdims=True))
    den = None
    acc = None
    for s, v in zip(s_list, v_list):
        e = jnp.exp2(s - m)
        d = e.sum(axis=-1, keepdims=True)
        o = jnp.dot(e.astype(BF16), v, preferred_element_type=F32)
        den = d if den is None else den + d
        acc = o if acc is None else acc + o
    return acc / den


def _gqa_kernel(*refs, tq, with_ctx):
    if with_ctx:
        (q_ref, kl_ref, vl_ref, kc_ref, vc_ref, g_ref, qc_ref, gc_ref,
         cos_q, sa_q, sb_q, cos_k, sa_k, sb_k, qn_ref, kn_ref,
         o_ref, oc_ref, k_sc, v_sc) = refs
    else:
        (q_ref, kl_ref, vl_ref, kc_ref, vc_ref, g_ref,
         cos_q, sa_q, sb_q, cos_k, sa_k, sb_k, qn_ref, kn_ref,
         o_ref, k_sc, v_sc) = refs
    qi = pl.program_id(2)
    qn = qn_ref[...] * (ATTN_SCALE * LOG2E)
    kn = kn_ref[...]

    @pl.when(qi == 0)
    def _():
        k_sc[0:CTX_LEN, :] = _head_rmsnorm(kc_ref[...], kn).astype(BF16)
        k_sc[CTX_LEN:, :] = _rope(_head_rmsnorm(kl_ref[...], kn),
                                  cos_k[...], sa_k[...], sb_k[...]).astype(BF16)
        v_sc[0:CTX_LEN, 0:HEAD_DIM] = vc_ref[...].astype(BF16)
        v_sc[CTX_LEN:, 0:HEAD_DIM] = vl_ref[...].astype(BF16)
        v_sc[:, HEAD_DIM:] = jnp.ones((CTX_LEN + SEQ, HEAD_DIM), BF16)
        if with_ctx:
            kc = k_sc[0:CTX_LEN, :]
            vc = v_sc[0:CTX_LEN, 0:HEAD_DIM]
            for h in range(Q_PER_KV):
                lanes = slice(h * HEAD_DIM, (h + 1) * HEAD_DIM)
                qh = _head_rmsnorm(qc_ref[:, lanes], qn).astype(BF16)
                o = _softmax_pv([_dot_nt(qh, kc)], [vc])
                oc_ref[:, lanes] = (o * _silu(gc_ref[:, lanes])).astype(oc_ref.dtype)

    k_all = k_sc[...]
    v_all = v_sc[...]
    for h in range(Q_PER_KV):
        lanes = slice(h * HEAD_DIM, (h + 1) * HEAD_DIM)
        for r in range(tq // GQA_ROWS):
            rows = slice(r * GQA_ROWS, (r + 1) * GQA_ROWS)
            qh = _rope(_head_rmsnorm(q_ref[rows, lanes], qn),
                       cos_q[rows, :], sa_q[rows, :], sb_q[rows, :]).astype(BF16)
            s = _dot_nt(qh, k_all)
            e = jnp.exp2(s - s.max(axis=-1, keepdims=True)).astype(BF16)
            o2 = jnp.dot(e, v_all, preferred_element_type=F32)
            o = o2[:, 0:HEAD_DIM] / o2[:, HEAD_DIM:HEAD_DIM + 1]
            o_ref[rows, lanes] = (o * _silu(g_ref[rows, lanes])).astype(o_ref.dtype)


GQA_ROWS = 256
GQA_TQ = 512


def _gqa_mixer(p_lat, p_ctx, ctx_cols, rope_tabs, qn3, kn3, l, with_ctx):
    tq = GQA_TQ
    ck, cv = ctx_cols["b_k"], ctx_cols["b_v"]
    nq = SEQ // tq
    qw = Q_PER_KV * HEAD_DIM
    cos, sin_a, sin_b = rope_tabs
    kern = functools.partial(_gqa_kernel, tq=tq, with_ctx=with_ctx)
    in_specs = [
        pl.BlockSpec((tq, qw), lambda b, kv, qi: (b * nq + qi, COL_B_Q // Q_PER_KV + kv)),
        pl.BlockSpec((SEQ, HEAD_DIM), lambda b, kv, qi: (b, COL_B_K + kv)),
        pl.BlockSpec((SEQ, HEAD_DIM), lambda b, kv, qi: (b, COL_B_V + kv)),
        pl.BlockSpec((CTX_LEN, HEAD_DIM), lambda b, kv, qi: (b, ck + kv)),
        pl.BlockSpec((CTX_LEN, HEAD_DIM), lambda b, kv, qi: (b, cv + kv)),
        pl.BlockSpec((tq, qw), lambda b, kv, qi: (b * nq + qi, COL_B_GATE // Q_PER_KV + kv)),
    ]
    args = [p_lat, p_lat, p_lat, p_ctx, p_ctx, p_lat]
    if with_ctx:
        in_specs += [
            pl.BlockSpec((CTX_LEN, qw), lambda b, kv, qi: (b, COL_B_Q // Q_PER_KV + kv)),
            pl.BlockSpec((CTX_LEN, qw), lambda b, kv, qi: (b, COL_B_GATE // Q_PER_KV + kv)),
        ]
        args += [p_ctx, p_ctx]
    tab_q = pl.BlockSpec((tq, HEAD_DIM), lambda b, kv, qi: (qi, 0))
    tab_k = pl.BlockSpec((SEQ, HEAD_DIM), lambda b, kv, qi: (0, 0))
    gvec = pl.BlockSpec((None, 1, HEAD_DIM), lambda b, kv, qi: (l, 0, 0))
    in_specs += [tab_q, tab_q, tab_q, tab_k, tab_k, tab_k, gvec, gvec]
    args += [cos, sin_a, sin_b, cos, sin_a, sin_b, qn3, kn3]
    out_specs = [pl.BlockSpec((tq, qw), lambda b, kv, qi: (b * nq + qi, kv))]
    out_shape = [jax.ShapeDtypeStruct((BATCH * SEQ, GROUP_W), BF16)]
    if with_ctx:
        out_specs.append(pl.BlockSpec((CTX_LEN, qw), lambda b, kv, qi: (b, kv)))
        out_shape.append(jax.ShapeDtypeStruct((BATCH * CTX_LEN, GROUP_W), BF16))
    res = pl.pallas_call(
        kern,
        grid=(BATCH, KV_HEADS, nq),
        in_specs=in_specs,
        out_specs=out_specs,
        out_shape=out_shape,
        scratch_shapes=[pltpu.VMEM((CTX_LEN + SEQ, HEAD_DIM), BF16),
                        pltpu.VMEM((CTX_LEN + SEQ, 2 * HEAD_DIM), BF16)],
        compiler_params=_params(("arbitrary", "arbitrary", "arbitrary")),
        name="gqa_mixer",
    )(*args)
    return (res[0], res[1]) if with_ctx else (res[0], None)


def _log_sigmoid(v):
    return -(jnp.maximum(-v, 0.0) + jnp.log1p(jnp.exp(-jnp.abs(v))))


RET_HEADS = 2
RET_UNROLL = 4


def _ret_kernel(*refs, with_ctx):
    if with_ctx:
        (ql_ref, kl_ref, vl_ref, gl_ref, qc_ref, kc_ref, vc_ref, gc_ref,
         cos_ref, sa_ref, sb_ref, decf_ref, decb_ref, gng_ref, gnb_ref,
         o_ref, oc_ref, q_sc, k_sc, v_sc, acc_sc, of_sc, ob_sc, qf_sc, qb_sc, kv_sc, cst_sc) = refs
    else:
        (ql_ref, kl_ref, vl_ref, gl_ref, kc_ref, vc_ref,
         cos_ref, sa_ref, sb_ref, decf_ref, decb_ref, gng_ref, gnb_ref,
         o_ref, q_sc, k_sc, v_sc, acc_sc, of_sc, ob_sc, qf_sc, qb_sc, kv_sc, cst_sc) = refs
        qc_ref = None

    n_ctx = CTX_LEN // CHUNK
    n_lat = SEQ // CHUNK
    n_all = n_ctx + n_lat
    ii = lax.broadcasted_iota(jnp.int32, (CHUNK, CHUNK), 0).astype(F32)
    jj = lax.broadcasted_iota(jnp.int32, (CHUNK, CHUNK), 1).astype(F32)
    dfw = ii - jj
    jrow = jj[0:1, :]
    cdec = []
    for hh in range(RET_HEADS):
        lanes = slice(hh * HEAD_DIM, (hh + 1) * HEAD_DIM)
        if with_ctx:
            q_sc[hh, 0:CTX_LEN, :] = qc_ref[:, lanes]
        q_sc[hh, CTX_LEN:, :] = _rope(ql_ref[:, lanes], cos_ref[...], sa_ref[...], sb_ref[...])
        k_sc[hh, 0:CTX_LEN, :] = kc_ref[:, lanes] * ATTN_SCALE
        k_sc[hh, CTX_LEN:, :] = _rope(kl_ref[:, lanes], cos_ref[...], sa_ref[...],
                                      sb_ref[...]) * ATTN_SCALE
        v_sc[hh, 0:CTX_LEN, :] = vc_ref[:, lanes].astype(BF16)
        v_sc[hh, CTX_LEN:, :] = vl_ref[:, lanes].astype(BF16)

        lg_f = _log_sigmoid(decf_ref[hh])
        lg_b = _log_sigmoid(decb_ref[hh])
        base = 4 * hh
        cst_sc[base + 0] = jnp.where(dfw >= 0, jnp.exp(jnp.maximum(dfw, 0.0) * lg_f), 0.0)
        cst_sc[base + 1] = jnp.where(dfw <= 0, jnp.exp(jnp.maximum(-dfw, 0.0) * lg_b), 0.0)
        cst_sc[base + 2] = jnp.exp((ii + 1.0) * lg_f)
        cst_sc[base + 3] = jnp.exp((CHUNK - ii) * lg_b)
        kdec_rows = (jnp.exp((CHUNK - 1.0 - jrow) * lg_f), jnp.exp(jrow * lg_b))
        cdec += [jnp.exp(CHUNK * lg_f), jnp.exp(CHUNK * lg_b)]

        def intra(c, carry, hh=hh, base=base, kdec_rows=kdec_rows, need_out=True):
            rows = pl.ds(pl.multiple_of(c * CHUNK, CHUNK), CHUNK)
            kc = k_sc[hh, rows, :]
            vc = v_sc[hh, rows, :]
            if need_out:
                qc = q_sc[hh, rows, :]
                a = _dot_nt(qc.astype(BF16), kc.astype(BF16))
                att = jnp.concatenate([a * cst_sc[base], a * cst_sc[base + 1]], axis=0)
                o2 = jnp.dot(att.astype(BF16), vc, preferred_element_type=F32)
                acc_sc[hh, rows, :] = o2[:CHUNK] + o2[CHUNK:]
                qf_sc[hh, rows, :] = (qc * cst_sc[base + 2]).astype(BF16)
                qb_sc[hh, rows, :] = (qc * cst_sc[base + 3]).astype(BF16)
            kt = jnp.transpose(kc)
            kd2 = jnp.concatenate([kt * kdec_rows[0], kt * kdec_rows[1]], axis=0)
            kv_sc[hh, c] = jnp.dot(kd2.astype(BF16), vc, preferred_element_type=F32)
            return carry

        for c in range(n_ctx):
            intra(c, 0, need_out=with_ctx)
        lax.fori_loop(n_ctx, n_all, intra, 0, unroll=RET_UNROLL)

    def inter(hh, direction, c, s, q_ref_, out_ref, need_out):
        if need_out:
            rows = pl.ds(pl.multiple_of(c * CHUNK, CHUNK), CHUNK)
            out_ref[hh, rows, :] = jnp.dot(q_ref_[hh, rows, :], s.astype(BF16),
                                           preferred_element_type=F32)
        kv = kv_sc[hh, c, direction * CHUNK:(direction + 1) * CHUNK, :]
        return s * cdec[2 * hh + direction] + kv

    states = []
    for hh in range(RET_HEADS):
        sf = jnp.zeros((HEAD_DIM, HEAD_DIM), F32)
        sb = jnp.zeros((HEAD_DIM, HEAD_DIM), F32)
        for c in range(n_ctx):
            sf = inter(hh, 0, c, sf, qf_sc, of_sc, with_ctx)
            sb = inter(hh, 1, n_ctx - 1 - c, sb, qb_sc, ob_sc, with_ctx)
        states += [sf, sb]

    def lat_step(t, carry):
        new = []
        for hh in range(RET_HEADS):
            new.append(inter(hh, 0, n_ctx + t, carry[2 * hh], qf_sc, of_sc, True))
            new.append(inter(hh, 1, n_all - 1 - t, carry[2 * hh + 1], qb_sc, ob_sc, True))
        return tuple(new)

    lax.fori_loop(0, n_lat, lat_step, tuple(states), unroll=RET_UNROLL)

    def group_norm(o, gate, lanes):
        mu = jnp.mean(o, axis=-1, keepdims=True)
        var = jnp.mean(jnp.square(o - mu), axis=-1, keepdims=True)
        on = (o - mu) * lax.rsqrt(var + LN_EPS)
        return (on * gng_ref[:, lanes] + gnb_ref[:, lanes]) * _silu(gate)

    for hh in range(RET_HEADS):
        lanes = slice(hh * HEAD_DIM, (hh + 1) * HEAD_DIM)
        o = acc_sc[hh, CTX_LEN:, :] + of_sc[hh, CTX_LEN:, :] + ob_sc[hh, CTX_LEN:, :]
        o_ref[:, lanes] = group_norm(o, gl_ref[:, lanes], lanes).astype(o_ref.dtype)
        if with_ctx:
            oc = acc_sc[hh, 0:CTX_LEN, :] + of_sc[hh, 0:CTX_LEN, :] + ob_sc[hh, 0:CTX_LEN, :]
            oc_ref[:, lanes] = group_norm(oc, gc_ref[:, lanes], lanes).astype(oc_ref.dtype)


def _retention_mixer(p_lat, p_ctx, ctx_cols, rope_tabs, dec_f, dec_b, gn_g3, gn_b3, l, with_ctx):
    cos, sin_a, sin_b = rope_tabs
    kern = functools.partial(_ret_kernel, with_ctx=with_ctx)
    width = RET_HEADS * HEAD_DIM

    def col(c0, rows):
        return pl.BlockSpec((rows, width), lambda b, h: (b, c0 // RET_HEADS + h))

    in_specs = [col(COL_C_Q, SEQ), col(COL_C_K, SEQ), col(COL_C_V, SEQ), col(COL_C_GATE, SEQ)]
    args = [p_lat, p_lat, p_lat, p_lat]
    if with_ctx:
        in_specs += [col(COL_C_Q, CTX_LEN), col(COL_C_K, CTX_LEN), col(COL_C_V, CTX_LEN),
                     col(COL_C_GATE, CTX_LEN)]
        args += [p_ctx, p_ctx, p_ctx, p_ctx]
    else:
        in_specs += [col(ctx_cols["c_k"], CTX_LEN), col(ctx_cols["c_v"], CTX_LEN)]
        args += [p_ctx, p_ctx]
    tab = pl.BlockSpec((SEQ, HEAD_DIM), lambda b, h: (0, 0))
    dec = pl.BlockSpec((None, RET_HEADS, 1, HEAD_DIM), lambda b, h: (l, h, 0, 0))
    gvec = pl.BlockSpec((None, 1, width), lambda b, h: (l, 0, h))
    in_specs += [tab, tab, tab, dec, dec, gvec, gvec]
    args += [cos, sin_a, sin_b, dec_f, dec_b, gn_g3, gn_b3]
    out_specs = [pl.BlockSpec((SEQ, width), lambda b, h: (b, h))]
    out_shape = [jax.ShapeDtypeStruct((BATCH * SEQ, GROUP_W), BF16)]
    if with_ctx:
        out_specs.append(pl.BlockSpec((CTX_LEN, width), lambda b, h: (b, h)))
        out_shape.append(jax.ShapeDtypeStruct((BATCH * CTX_LEN, GROUP_W), BF16))
    n_rows = CTX_LEN + SEQ
    rows3 = (RET_HEADS, n_rows, HEAD_DIM)
    res = pl.pallas_call(
        kern,
        grid=(BATCH, N_HEADS // RET_HEADS),
        in_specs=in_specs,
        out_specs=out_specs,
        out_shape=out_shape,
        scratch_shapes=[pltpu.VMEM(rows3, F32), pltpu.VMEM(rows3, F32), pltpu.VMEM(rows3, BF16),
                        pltpu.VMEM(rows3, F32), pltpu.VMEM(rows3, F32), pltpu.VMEM(rows3, F32),
                        pltpu.VMEM(rows3, BF16), pltpu.VMEM(rows3, BF16),
                        pltpu.VMEM((RET_HEADS, n_rows // CHUNK, 2 * CHUNK, HEAD_DIM), F32),
                        pltpu.VMEM((RET_HEADS * 4, CHUNK, CHUNK), F32)],
        compiler_params=_params(("arbitrary", "arbitrary")),
        name="retention_mixer",
    )(*args)
    return (res[0], res[1]) if with_ctx else (res[0], None)


def _na_row_start(r):
    return min(max(r - NA_WIN_R // 2, 0), GRID_H - NA_WIN_R)


def _na_key_start(blk):
    return min(max(blk * NA_BLK_ROWS - NA_WIN_R // 2, 0), GRID_H - NA_BLK_KROWS)


def _na_table_kernel(tab_ref, bias_ref, mask_ref):
    q = lax.broadcasted_iota(jnp.int32, (GRID_W, HEAD_DIM), 0)
    lane = lax.broadcasted_iota(jnp.int32, (GRID_W, HEAD_DIM), 1)
    kcol = lane % GRID_W
    lower_half = lane < GRID_W
    off_c = jnp.clip(kcol - q, -(NA_WIN_C - 1), NA_WIN_C - 1) + (NA_WIN_C - 1)
    c_start = jnp.clip(q - NA_WIN_C // 2, 0, GRID_W - NA_WIN_C)
    col_win = jnp.where((kcol >= c_start) & (kcol < c_start + NA_WIN_C), 1.0, 0.0)
    zeros = jnp.zeros((GRID_W, HEAD_DIM), F32)
    planes = []
    for o in range(2 * NA_WIN_R - 1):
        g = zeros
        for c in range(2 * NA_WIN_C - 1):
            g = jnp.where(off_c == c, tab_ref[0, o, c] * LOG2E, g)
        planes.append(g)
    n_blk = GRID_H // NA_BLK_ROWS
    for pat, blk in enumerate((0, 1, n_blk - 1)):
        k0 = _na_key_start(blk)
        for rq in range(NA_BLK_ROWS):
            r = blk * NA_BLK_ROWS + rq
            rs = _na_row_start(r)
            for jp in range(NA_BLK_KROWS // 2):
                b_half, m_half = [], []
                for key_row in (k0 + 2 * jp, k0 + 2 * jp + 1):
                    valid = rs <= key_row < rs + NA_WIN_R
                    b_half.append(planes[key_row - r + NA_WIN_R - 1] if valid else zeros)
                    m_half.append(col_win if valid else zeros)
                rows = slice(rq * GRID_W, (rq + 1) * GRID_W)
                lanes = slice(jp * HEAD_DIM, (jp + 1) * HEAD_DIM)
                bias_ref[0, pat, rows, lanes] = jnp.where(lower_half, b_half[0], b_half[1])
                mask_ref[0, pat, rows, lanes] = jnp.where(lower_half, m_half[0], m_half[1])


def _na_tables(na_bias_l):
    spec = pl.BlockSpec((1, NA_PATTERNS, NA_BLK_Q, NA_BLK_KEYS), lambda h: (h, 0, 0, 0))
    shape = jax.ShapeDtypeStruct((N_HEADS, NA_PATTERNS, NA_BLK_Q, NA_BLK_KEYS), F32)
    return pl.pallas_call(
        _na_table_kernel,
        grid=(N_HEADS,),
        in_specs=[pl.BlockSpec((1, 2 * NA_WIN_R - 1, 2 * NA_WIN_C - 1), lambda h: (h, 0, 0),
                               memory_space=pltpu.SMEM)],
        out_specs=[spec, spec],
        out_shape=[shape, shape],
        compiler_params=_params(("arbitrary",)),
        name="na_tables",
    )(na_bias_l)


def _na_kernel(*refs, with_ctx):
    if with_ctx:
        (q_ref, k_ref, v_ref, g_ref, qc_ref, kc_ref, vc_ref, gc_ref, bias_ref, mask_ref,
         o_ref, oc_ref, q_sc, k_sc, v_sc, scx_sc) = refs
    else:
        (q_ref, k_ref, v_ref, g_ref, kc_ref, vc_ref, bias_ref, mask_ref,
         o_ref, q_sc, k_sc, v_sc, scx_sc) = refs
    scale = ATTN_SCALE * LOG2E
    q_sc[...] = (q_ref[...] * scale).astype(BF16)
    k_sc[...] = k_ref[...].astype(BF16)
    v_sc[...] = v_ref[...].astype(BF16)
    kc = kc_ref[...].astype(BF16)
    vc = vc_ref[...].astype(BF16)
    scx_sc[...] = _dot_nt(q_sc[...], kc)
    n_blk = GRID_H // NA_BLK_ROWS

    def blk_step(blk, carry):
        k0 = jnp.clip(blk * NA_BLK_ROWS - NA_WIN_R // 2, 0, GRID_H - NA_BLK_KROWS)
        pat = jnp.where(blk == 0, 0, jnp.where(blk == n_blk - 1, 2, 1))
        rows = pl.ds(pl.multiple_of(blk * NA_BLK_Q, NA_BLK_Q), NA_BLK_Q)
        keys = pl.ds(pl.multiple_of(k0 * GRID_W, GRID_W), NA_BLK_KEYS)
        s_nb = _dot_nt(q_sc[rows, :], k_sc[keys, :]) + bias_ref[0, pat]
        s_nb = jnp.where(mask_ref[0, pat] > 0.5, s_nb, MASK_VALUE)
        o = _softmax_pv([s_nb, scx_sc[rows, :]], [v_sc[keys, :], vc])
        o_ref[rows, :] = (o * _silu(g_ref[rows, :])).astype(o_ref.dtype)
        return carry

    lax.fori_loop(0, n_blk, blk_step, 0, unroll=NA_UNROLL)

    if with_ctx:
        s = _dot_nt((qc_ref[...] * scale).astype(BF16), kc)
        o = _softmax_pv([s], [vc])
        oc_ref[...] = (o * _silu(gc_ref[...])).astype(oc_ref.dtype)


def _na_mixer(p_lat, p_ctx, ctx_cols, na_tabs, with_ctx):
    kern = functools.partial(_na_kernel, with_ctx=with_ctx)

    def col(c0, rows):
        return pl.BlockSpec((rows, HEAD_DIM), lambda b, h: (b, c0 + h))

    in_specs = [col(COL_D_Q, SEQ), col(COL_D_K, SEQ), col(COL_D_V, SEQ), col(COL_D_GATE, SEQ)]
    args = [p_lat, p_lat, p_lat, p_lat]
    if with_ctx:
        in_specs += [col(COL_D_Q, CTX_LEN), col(COL_D_K, CTX_LEN), col(COL_D_V, CTX_LEN),
                     col(COL_D_GATE, CTX_LEN)]
        args += [p_ctx, p_ctx, p_ctx, p_ctx]
    else:
        in_specs += [col(ctx_cols["d_k"], CTX_LEN), col(ctx_cols["d_v"], CTX_LEN)]
        args += [p_ctx, p_ctx]
    tab = pl.BlockSpec((1, NA_PATTERNS, NA_BLK_Q, NA_BLK_KEYS), lambda b, h: (h, 0, 0, 0))
    in_specs += [tab, tab]
    args += list(na_tabs)
    out_specs = [pl.BlockSpec((SEQ, HEAD_DIM), lambda b, h: (b, h))]
    out_shape = [jax.ShapeDtypeStruct((BATCH * SEQ, GROUP_W), BF16)]
    if with_ctx:
        out_specs.append(pl.BlockSpec((CTX_LEN, HEAD_DIM), lambda b, h: (b, h)))
        out_shape.append(jax.ShapeDtypeStruct((BATCH * CTX_LEN, GROUP_W), BF16))
    res = pl.pallas_call(
        kern,
        grid=(BATCH, N_HEADS),
        in_specs=in_specs,
        out_specs=out_specs,
        out_shape=out_shape,
        scratch_shapes=[pltpu.VMEM((SEQ, HEAD_DIM), BF16), pltpu.VMEM((SEQ, HEAD_DIM), BF16),
                        pltpu.VMEM((SEQ, HEAD_DIM), BF16), pltpu.VMEM((SEQ, CTX_LEN), F32)],
        compiler_params=_params(("arbitrary", "arbitrary")),
        name="na_mixer",
    )(*args)
    return (res[0], res[1]) if with_ctx else (res[0], None)


def _rope_tables():
    t = jnp.arange(SEQ)
    row = (t // GRID_W).astype(F32)
    col = (t % GRID_W).astype(F32)
    half = HEAD_DIM // 2
    inv_freq = ROPE_THETA ** (-jnp.arange(0, half, 2, dtype=F32) / half)
    ang_r = row[:, None] * inv_freq[None, :]
    ang_c = col[:, None] * inv_freq[None, :]
    ang = jnp.concatenate([ang_r, ang_r, ang_c, ang_c], axis=-1)
    cos, sin = jnp.cos(ang), jnp.sin(ang)
    quarter = (jnp.arange(HEAD_DIM) // (HEAD_DIM // 4)) % 2
    sin_a = jnp.where(quarter[None, :] == 0, -sin, 0.0)
    sin_b = jnp.where(quarter[None, :] == 1, sin, 0.0)
    return cos, sin_a, sin_b


def kernel(x, c, ctx, c_ctx, ada_w, ada_b, norm_g, w_in, conv_w, conv_b, conv_ln_g, conv_ln_b,
           gqa_qn_g, gqa_kn_g, ret_decay_fwd, ret_decay_bwd, ret_gn_g, ret_gn_b, na_bias,
           w_out, final_g):
    rope_tabs = _rope_tables()
    x_lat = x.reshape(BATCH * SEQ, D_MODEL)
    x_ctx = ctx.reshape(BATCH * CTX_LEN, D_MODEL)
    c8 = jnp.concatenate(
        [c, c_ctx[None, :], jnp.zeros((MOD_ROWS - BATCH - 1, D_MODEL), F32)], axis=0)

    def vec3(a):
        return a.reshape(DEPTH, 1, a.shape[-1])

    ada_b3, norm_g3 = vec3(ada_b), vec3(norm_g)
    conv_b3, ln_g3, ln_b3 = vec3(conv_b), vec3(conv_ln_g), vec3(conv_ln_b)
    qn3, kn3 = vec3(gqa_qn_g), vec3(gqa_kn_g)
    gn_g3, gn_b3 = vec3(ret_gn_g), vec3(ret_gn_b)
    dec_f = jnp.broadcast_to(ret_decay_fwd[:, :, None, None], (DEPTH, N_HEADS, 1, HEAD_DIM))
    dec_b = jnp.broadcast_to(ret_decay_bwd[:, :, None, None], (DEPTH, N_HEADS, 1, HEAD_DIM))

    for l in range(DEPTH):
        with_ctx = l < DEPTH - 1
        mod = _ada_table(c8, ada_w, ada_b3, l)
        h_lat = _norm_modulate(x_lat, norm_g3, l, mod, SEQ, None)
        h_ctx = _norm_modulate(x_ctx, norm_g3, l, mod, None, MOD_CTX_ROW)
        p_lat = _in_proj(h_lat, w_in, l)
        p_ctx = _in_proj(h_ctx, w_in, l, kv_only=not with_ctx)
        ctx_cols = CTX_COLS_FULL if with_ctx else CTX_COLS_KV_ONLY

        ya_l = _conv_mixer(p_lat, SEQ, conv_w, conv_b3, ln_g3, ln_b3, l)
        yb_l, yb_c = _gqa_mixer(p_lat, p_ctx, ctx_cols, rope_tabs, qn3, kn3, l, with_ctx)
        yc_l, yc_c = _retention_mixer(p_lat, p_ctx, ctx_cols, rope_tabs, dec_f, dec_b, gn_g3, gn_b3,
                                      l, with_ctx)
        yd_l, yd_c = _na_mixer(p_lat, p_ctx, ctx_cols, _na_tables(na_bias[l]), with_ctx)

        x_new = _out_proj((ya_l, yb_l, yc_l, yd_l), w_out, l, x_lat, mod, SEQ, None)
        if with_ctx:
            ya_c = _conv_mixer(p_ctx, CTX_LEN, conv_w, conv_b3, ln_g3, ln_b3, l)
            x_ctx = _out_proj((ya_c, yb_c, yc_c, yd_c), w_out, l, x_ctx, mod, None, MOD_CTX_ROW)
        x_lat = x_new

    out = _final_norm(x_lat, final_g.reshape(1, D_MODEL))
    return out.reshape(BATCH, SEQ, D_MODEL)
```

```python
import functools

import jax
import jax.numpy as jnp
from jax import lax
from jax.experimental import pallas as pl
from jax.experimental.pallas import tpu as pltpu

F32 = jnp.float32
BF16 = jnp.bfloat16

D_MODEL = 4096
BATCH = 4
SEQ = 2048
DEPTH = 2
GRID_W = 64
GRID_H = SEQ // GRID_W
CTX_LEN = 256
GROUP_W = 1024
HEAD_DIM = 128
N_HEADS = GROUP_W // HEAD_DIM
KV_HEADS = 2
Q_PER_KV = N_HEADS // KV_HEADS
CONV_WIDTH = 31
CONV_HALF = CONV_WIDTH // 2
CHUNK = 128
NA_WIN_R = 8
NA_WIN_C = 16
ROPE_THETA = 10000.0
NORM_EPS = 1e-6
LN_EPS = 1e-5
ATTN_SCALE = HEAD_DIM ** -0.5
MASK_VALUE = -1e30
P_TOTAL = 13 * GROUP_W + 2 * KV_HEADS * HEAD_DIM

COL_A_GLU = 0
COL_A_GATE = 16
COL_B_Q = 24
COL_B_K = 32
COL_B_V = 34
COL_B_GATE = 36
COL_C_Q = 44
COL_C_K = 52
COL_C_V = 60
COL_C_GATE = 68
COL_D_Q = 76
COL_D_K = 84
COL_D_V = 92
COL_D_GATE = 100

NORM_TM = 512
MOD_ROWS = 8
MOD_CTX_ROW = BATCH
VMEM_LIMIT = 56 * 1024 * 1024

NA_BLK_ROWS = 4
NA_BLK_KROWS = 12
NA_BLK_Q = NA_BLK_ROWS * GRID_W
NA_BLK_KEYS = NA_BLK_KROWS * GRID_W
NA_PATTERNS = 3
NA_UNROLL = 4


def _params(sem, flags=None):
    return pltpu.CompilerParams(dimension_semantics=sem, vmem_limit_bytes=VMEM_LIMIT, flags=flags)


def _silu(v):
    return v * jax.nn.sigmoid(v)


def _mod_kernel(c_ref, w_ref, b_ref, o_ref):
    a = _silu(c_ref[...]).astype(BF16)
    o_ref[...] = jnp.dot(a, w_ref[...].astype(BF16), preferred_element_type=F32) + b_ref[...]


def _ada_table(c8, ada_w, ada_b3, l):
    tn = 512
    n = 3 * D_MODEL
    return pl.pallas_call(
        _mod_kernel,
        grid=(n // tn,),
        in_specs=[
            pl.BlockSpec((MOD_ROWS, D_MODEL), lambda j: (0, 0)),
            pl.BlockSpec((None, D_MODEL, tn), lambda j: (l, 0, j)),
            pl.BlockSpec((None, 1, tn), lambda j: (l, 0, j)),
        ],
        out_specs=pl.BlockSpec((MOD_ROWS, tn), lambda j: (0, j)),
        out_shape=jax.ShapeDtypeStruct((MOD_ROWS, n), F32),
        compiler_params=_params(("arbitrary",)),
        name="ada_table",
    )(c8, ada_w, ada_b3)


def _normmod_kernel(x_ref, g_ref, shift_ref, scale_ref, o_ref, *, tiles_per_row, fixed_row):
    if fixed_row is None:
        r = pl.program_id(0) // tiles_per_row
    else:
        r = fixed_row
    x = x_ref[...]
    y = x * lax.rsqrt(jnp.mean(x * x, axis=-1, keepdims=True) + NORM_EPS)
    y = y * g_ref[...]
    shift = shift_ref[pl.ds(r, 1), :]
    scale = scale_ref[pl.ds(r, 1), :]
    o_ref[...] = (y * (1.0 + scale) + shift).astype(o_ref.dtype)


def _norm_modulate(x2, norm_g3, l, mod, rows_per_mod, fixed_row):
    m = x2.shape[0]
    tm = NORM_TM
    tiles_per_row = None if rows_per_mod is None else rows_per_mod // tm
    kern = functools.partial(_normmod_kernel, tiles_per_row=tiles_per_row, fixed_row=fixed_row)
    return pl.pallas_call(
        kern,
        grid=(m // tm,),
        in_specs=[
            pl.BlockSpec((tm, D_MODEL), lambda i: (i, 0)),
            pl.BlockSpec((None, 1, D_MODEL), lambda i: (l, 0, 0)),
            pl.BlockSpec((MOD_ROWS, D_MODEL), lambda i: (0, 0)),
            pl.BlockSpec((MOD_ROWS, D_MODEL), lambda i: (0, 1)),
        ],
        out_specs=pl.BlockSpec((tm, D_MODEL), lambda i: (i, 0)),
        out_shape=jax.ShapeDtypeStruct((m, D_MODEL), BF16),
        compiler_params=_params(("arbitrary",)),
        name="norm_modulate",
    )(x2, norm_g3, mod, mod)


PROJ_TM = 1024
PROJ_TN = 512


def _stage_weight_chunk(w_ref, wbf_ref, n_cols):
    j, i = pl.program_id(0), pl.program_id(1)
    chunk = w_ref.shape[0]

    @pl.when(j < n_cols)
    def _():
        wbf_ref[j % 2, pl.ds(pl.multiple_of(i * chunk, chunk), chunk), :] = w_ref[...].astype(BF16)

    return j, (j + 1) % 2


def _mm_kernel(a_ref, w_ref, o_ref, wbf_ref, *, n_cols):
    j, slot = _stage_weight_chunk(w_ref, wbf_ref, n_cols)

    @pl.when(j > 0)
    def _():
        o_ref[...] = jnp.dot(a_ref[...], wbf_ref[slot], preferred_element_type=F32)


def _ctx_kv_col_tile(j):
    b_kv = COL_B_K * HEAD_DIM // PROJ_TN
    c_kv = COL_C_K * HEAD_DIM // PROJ_TN
    d_kv = COL_D_K * HEAD_DIM // PROJ_TN
    n_c = 2 * GROUP_W // PROJ_TN
    return jnp.where(j < 1, b_kv, jnp.where(j < 1 + n_c, c_kv + (j - 1), d_kv + (j - 1 - n_c)))


CTX_KV_TILES = 1 + 2 * (2 * GROUP_W // PROJ_TN)
CTX_COLS_FULL = dict(b_k=COL_B_K, b_v=COL_B_V, c_k=COL_C_K, c_v=COL_C_V, d_k=COL_D_K, d_v=COL_D_V)
CTX_COLS_KV_ONLY = dict(b_k=0, b_v=KV_HEADS, c_k=2 * KV_HEADS, c_v=2 * KV_HEADS + N_HEADS,
                        d_k=2 * KV_HEADS + 2 * N_HEADS, d_v=2 * KV_HEADS + 3 * N_HEADS)


def _in_proj(h, w_in, l, kv_only=False):
    m = h.shape[0]
    tm, tn = PROJ_TM, PROJ_TN
    n_tiles = CTX_KV_TILES if kv_only else P_TOTAL // tn
    col = _ctx_kv_col_tile if kv_only else (lambda j: j)
    n_rows = m // tm
    chunk = D_MODEL // n_rows
    return pl.pallas_call(
        functools.partial(_mm_kernel, n_cols=n_tiles),
        grid=(n_tiles + 1, n_rows),
        in_specs=[
            pl.BlockSpec((tm, D_MODEL), lambda j, i: (jnp.where(j == 0, 0, i), 0)),
            pl.BlockSpec((None, chunk, tn), lambda j, i: (l, i, col(jnp.minimum(j, n_tiles - 1)))),
        ],
        out_specs=pl.BlockSpec((tm, tn), lambda j, i: (jnp.where(j == 0, 0, i), jnp.maximum(j - 1, 0))),
        out_shape=jax.ShapeDtypeStruct((m, n_tiles * tn), F32),
        scratch_shapes=[pltpu.VMEM((2, D_MODEL, tn), BF16)],
        compiler_params=_params(("arbitrary", "arbitrary")),
        name="in_proj",
    )(h, w_in)


def _out_kernel(ya_ref, yb_ref, yc_ref, yd_ref, w_ref, x_ref, gate_ref, o_ref, wbf_ref, *,
                tiles_per_row, fixed_row, n_cols):
    if fixed_row is None:
        r = pl.program_id(1) // tiles_per_row
    else:
        r = fixed_row
    j, slot = _stage_weight_chunk(w_ref, wbf_ref, n_cols)

    @pl.when(j > 0)
    def _():
        acc = None
        for g, y_ref in enumerate((ya_ref, yb_ref, yc_ref, yd_ref)):
            part = jnp.dot(y_ref[...], wbf_ref[slot, g * GROUP_W:(g + 1) * GROUP_W, :],
                           preferred_element_type=F32)
            acc = part if acc is None else acc + part
        gate = gate_ref[pl.ds(r, 1), :]
        o_ref[...] = x_ref[...] + gate * acc


def _out_proj(ys, w_out, l, x2, mod, rows_per_mod, fixed_row):
    m = x2.shape[0]
    tm, tn = PROJ_TM, PROJ_TN
    tiles_per_row = None if rows_per_mod is None else rows_per_mod // tm
    n_cols = D_MODEL // tn
    n_rows = m // tm
    chunk = D_MODEL // n_rows
    kern = functools.partial(_out_kernel, tiles_per_row=tiles_per_row, fixed_row=fixed_row,
                             n_cols=n_cols)

    def out_col(j):
        return jnp.maximum(j - 1, 0)

    def row(j, i):
        return jnp.where(j == 0, 0, i)

    y_spec = pl.BlockSpec((tm, GROUP_W), lambda j, i: (row(j, i), 0))
    gate_col0 = 2 * D_MODEL // tn
    return pl.pallas_call(
        kern,
        grid=(n_cols + 1, n_rows),
        in_specs=[
            y_spec, y_spec, y_spec, y_spec,
            pl.BlockSpec((None, chunk, tn), lambda j, i: (l, i, jnp.minimum(j, n_cols - 1))),
            pl.BlockSpec((tm, tn), lambda j, i: (row(j, i), out_col(j))),
            pl.BlockSpec((MOD_ROWS, tn), lambda j, i: (0, gate_col0 + out_col(j))),
        ],
        out_specs=pl.BlockSpec((tm, tn), lambda j, i: (row(j, i), out_col(j))),
        out_shape=jax.ShapeDtypeStruct((m, D_MODEL), F32),
        scratch_shapes=[pltpu.VMEM((2, D_MODEL, tn), BF16)],
        compiler_params=_params(("arbitrary", "arbitrary")),
        name="out_proj",
    )(*ys, w_out, x2, mod)


def _final_norm_kernel(x_ref, g_ref, o_ref):
    x = x_ref[...]
    o_ref[...] = x * lax.rsqrt(jnp.mean(x * x, axis=-1, keepdims=True) + NORM_EPS) * g_ref[...]


def _final_norm(x2, g2):
    m = x2.shape[0]
    tm = NORM_TM
    return pl.pallas_call(
        _final_norm_kernel,
        grid=(m // tm,),
        in_specs=[pl.BlockSpec((tm, D_MODEL), lambda i: (i, 0)),
                  pl.BlockSpec((1, D_MODEL), lambda i: (0, 0))],
        out_specs=pl.BlockSpec((tm, D_MODEL), lambda i: (i, 0)),
        out_shape=jax.ShapeDtypeStruct((m, D_MODEL), F32),
        compiler_params=_params(("arbitrary",)),
        name="final_norm",
    )(x2, g2)


CONV_HALO = 16
CONV_SUB = 64
SUBLANES = 8


def _conv_kernel(cur_ref, prev_ref, next_ref, gate_ref, w_ref, b_ref, lng_ref, lnb_ref, o_ref,
                 u_ref, ush_ref, c_ref, *, tile, n_tiles):
    t = pl.program_id(1)

    def glu(ref):
        return ref[:, :GROUP_W] * jax.nn.sigmoid(ref[:, GROUP_W:])

    u_ref[CONV_HALO:CONV_HALO + tile, :] = glu(cur_ref)
    u_ref[0:CONV_HALO, :] = jnp.where(t > 0, glu(prev_ref), 0.0)
    u_ref[CONV_HALO + tile:, :] = jnp.where(t < n_tiles - 1, glu(next_ref), 0.0)

    sh_rows = ush_ref.shape[1]
    for b in range(1, SUBLANES):
        ush_ref[b - 1] = u_ref[b:b + sh_rows, :]

    base = CONV_HALO - CONV_HALF

    for s in range(tile // CONV_SUB):
        row0 = s * CONV_SUB
        for c in range(GROUP_W // HEAD_DIM):
            lanes = slice(c * HEAD_DIM, (c + 1) * HEAD_DIM)
            acc = jnp.zeros((CONV_SUB, HEAD_DIM), F32)
            for k in range(CONV_WIDTH):
                b = (base + k) % SUBLANES
                rows = slice(row0 + base + k - b, row0 + base + k - b + CONV_SUB)
                tap = u_ref[rows, lanes] if b == 0 else ush_ref[b - 1, rows, lanes]
                acc = acc + w_ref[k:k + 1, lanes] * tap
            c_ref[row0:row0 + CONV_SUB, lanes] = acc + b_ref[:, lanes]

    v = c_ref[...]
    mu = jnp.mean(v, axis=-1, keepdims=True)
    var = jnp.mean(jnp.square(v - mu), axis=-1, keepdims=True)
    y = (v - mu) * lax.rsqrt(var + LN_EPS) * lng_ref[...] + lnb_ref[...]
    o_ref[...] = (_silu(y) * _silu(gate_ref[...])).astype(o_ref.dtype)


def _conv_mixer(p, seq_len, conv_w, conv_b3, ln_g3, ln_b3, l):
    tile = 256
    n_tiles = seq_len // tile
    n_seq = p.shape[0] // seq_len
    halo_per_tile = tile // CONV_HALO
    last_halo = p.shape[0] // CONV_HALO - 1
    kern = functools.partial(_conv_kernel, tile=tile, n_tiles=n_tiles)
    vec = pl.BlockSpec((None, 1, GROUP_W), lambda b, t: (l, 0, 0))
    return pl.pallas_call(
        kern,
        grid=(n_seq, n_tiles),
        in_specs=[
            pl.BlockSpec((tile, 2 * GROUP_W), lambda b, t: (b * n_tiles + t, 0)),
            pl.BlockSpec((CONV_HALO, 2 * GROUP_W),
                         lambda b, t: (jnp.maximum((b * n_tiles + t) * halo_per_tile - 1, 0), 0)),
            pl.BlockSpec((CONV_HALO, 2 * GROUP_W),
                         lambda b, t: (jnp.minimum((b * n_tiles + t + 1) * halo_per_tile, last_halo), 0)),
            pl.BlockSpec((tile, GROUP_W), lambda b, t: (b * n_tiles + t, COL_A_GATE // 8)),
            pl.BlockSpec((None, CONV_WIDTH, GROUP_W), lambda b, t: (l, 0, 0)),
            vec, vec, vec,
        ],
        out_specs=pl.BlockSpec((tile, GROUP_W), lambda b, t: (b * n_tiles + t, 0)),
        out_shape=jax.ShapeDtypeStruct((p.shape[0], GROUP_W), BF16),
        scratch_shapes=[pltpu.VMEM((tile + 2 * CONV_HALO, GROUP_W), F32),
                        pltpu.VMEM((SUBLANES - 1, tile + 2 * CONV_HALO - SUBLANES, GROUP_W), F32),
                        pltpu.VMEM((tile, GROUP_W), F32)],
        compiler_params=_params(("arbitrary", "arbitrary")),
        name="conv_mixer",
    )(p, p, p, p, conv_w, conv_b3, ln_g3, ln_b3)


def _head_rmsnorm(v, g):
    return v * lax.rsqrt(jnp.mean(v * v, axis=-1, keepdims=True) + NORM_EPS) * g


def _rope(v, cos, sin_a, sin_b):
    return (v * cos + pltpu.roll(v, HEAD_DIM - HEAD_DIM // 4, 1) * sin_a
            + pltpu.roll(v, HEAD_DIM // 4, 1) * sin_b)


def _dot_nt(a, b):
    return lax.dot_general(a, b, (((1,), (1,)), ((), ())), preferred_element_type=F32)


LOG2E = 1.4426950408889634


def _softmax_pv(s_list, v_list):
    m = s_list[0].max(axis=-1, keepdims=True)
    for s in s_list[1:]:
        m = jnp.maximum(m, s.max(axis=-1, keepdims=True))
    den = None
    acc = None
    for s, v in zip(s_list, v_list):
        e = jnp.exp2(s - m)
        d = e.sum(axis=-1, keepdims=True)
        o = jnp.dot(e.astype(BF16), v, preferred_element_type=F32)
        den = d if den is None else den + d
        acc = o if acc is None else acc + o
    return acc / den


def _gqa_kernel(*refs, tq, with_ctx):
    if with_ctx:
        (q_ref, kl_ref, vl_ref, kc_ref, vc_ref, g_ref, qc_ref, gc_ref,
         cos_q, sa_q, sb_q, cos_k, sa_k, sb_k, qn_ref, kn_ref,
         o_ref, oc_ref, k_sc, v_sc) = refs
    else:
        (q_ref, kl_ref, vl_ref, kc_ref, vc_ref, g_ref,
         cos_q, sa_q, sb_q, cos_k, sa_k, sb_k, qn_ref, kn_ref,
         o_ref, k_sc, v_sc) = refs
    qi = pl.program_id(2)
    qn = qn_ref[...] * (ATTN_SCALE * LOG2E)
    kn = kn_ref[...]

    @pl.when(qi == 0)
    def _():
        k_sc[0:CTX_LEN, :] = _head_rmsnorm(kc_ref[...], kn).astype(BF16)
        k_sc[CTX_LEN:, :] = _rope(_head_rmsnorm(kl_ref[...], kn),
                                  cos_k[...], sa_k[...], sb_k[...]).astype(BF16)
        v_sc[0:CTX_LEN, 0:HEAD_DIM] = vc_ref[...].astype(BF16)
        v_sc[CTX_LEN:, 0:HEAD_DIM] = vl_ref[...].astype(BF16)
        v_sc[:, HEAD_DIM:] = jnp.ones((CTX_LEN + SEQ, HEAD_DIM), BF16)
        if with_ctx:
            kc = k_sc[0:CTX_LEN, :]
            vc = v_sc[0:CTX_LEN, 0:HEAD_DIM]
            for h in range(Q_PER_KV):
                lanes = slice(h * HEAD_DIM, (h + 1) * HEAD_DIM)
                qh = _head_rmsnorm(qc_ref[:, lanes], qn).astype(BF16)
                o = _softmax_pv([_dot_nt(qh, kc)], [vc])
                oc_ref[:, lanes] = (o * _silu(gc_ref[:, lanes])).astype(oc_ref.dtype)

    k_all = k_sc[...]
    v_all = v_sc[...]
    for h in range(Q_PER_KV):
        lanes = slice(h * HEAD_DIM, (h + 1) * HEAD_DIM)
        for r in range(tq // GQA_ROWS):
            rows = slice(r * GQA_ROWS, (r + 1) * GQA_ROWS)
            qh = _rope(_head_rmsnorm(q_ref[rows, lanes], qn),
                       cos_q[rows, :], sa_q[rows, :], sb_q[rows, :]).astype(BF16)
            s = _dot_nt(qh, k_all)
            e = jnp.exp2(s - s.max(axis=-1, keepdims=True)).astype(BF16)
            o2 = jnp.dot(e, v_all, preferred_element_type=F32)
            o = o2[:, 0:HEAD_DIM] / o2[:, HEAD_DIM:HEAD_DIM + 1]
            o_ref[rows, lanes] = (o * _silu(g_ref[rows, lanes])).astype(o_ref.dtype)


GQA_ROWS = 256
GQA_TQ = 1024


def _gqa_mixer(p_lat, p_ctx, ctx_cols, rope_tabs, qn3, kn3, l, with_ctx):
    tq = GQA_TQ
    ck, cv = ctx_cols["b_k"], ctx_cols["b_v"]
    nq = SEQ // tq
    qw = Q_PER_KV * HEAD_DIM
    cos, sin_a, sin_b = rope_tabs
    kern = functools.partial(_gqa_kernel, tq=tq, with_ctx=with_ctx)
    in_specs = [
        pl.BlockSpec((tq, qw), lambda b, kv, qi: (b * nq + qi, COL_B_Q // Q_PER_KV + kv)),
        pl.BlockSpec((SEQ, HEAD_DIM), lambda b, kv, qi: (b, COL_B_K + kv)),
        pl.BlockSpec((SEQ, HEAD_DIM), lambda b, kv, qi: (b, COL_B_V + kv)),
        pl.BlockSpec((CTX_LEN, HEAD_DIM), lambda b, kv, qi: (b, ck + kv)),
        pl.BlockSpec((CTX_LEN, HEAD_DIM), lambda b, kv, qi: (b, cv + kv)),
        pl.BlockSpec((tq, qw), lambda b, kv, qi: (b * nq + qi, COL_B_GATE // Q_PER_KV + kv)),
    ]
    args = [p_lat, p_lat, p_lat, p_ctx, p_ctx, p_lat]
    if with_ctx:
        in_specs += [
            pl.BlockSpec((CTX_LEN, qw), lambda b, kv, qi: (b, COL_B_Q // Q_PER_KV + kv)),
            pl.BlockSpec((CTX_LEN, qw), lambda b, kv, qi: (b, COL_B_GATE // Q_PER_KV + kv)),
        ]
        args += [p_ctx, p_ctx]
    tab_q = pl.BlockSpec((tq, HEAD_DIM), lambda b, kv, qi: (qi, 0))
    tab_k = pl.BlockSpec((SEQ, HEAD_DIM), lambda b, kv, qi: (0, 0))
    gvec = pl.BlockSpec((None, 1, HEAD_DIM), lambda b, kv, qi: (l, 0, 0))
    in_specs += [tab_q, tab_q, tab_q, tab_k, tab_k, tab_k, gvec, gvec]
    args += [cos, sin_a, sin_b, cos, sin_a, sin_b, qn3, kn3]
    out_specs = [pl.BlockSpec((tq, qw), lambda b, kv, qi: (b * nq + qi, kv))]
    out_shape = [jax.ShapeDtypeStruct((BATCH * SEQ, GROUP_W), BF16)]
    if with_ctx:
        out_specs.append(pl.BlockSpec((CTX_LEN, qw), lambda b, kv, qi: (b, kv)))
        out_shape.append(jax.ShapeDtypeStruct((BATCH * CTX_LEN, GROUP_W), BF16))
    res = pl.pallas_call(
        kern,
        grid=(BATCH, KV_HEADS, nq),
        in_specs=in_specs,
        out_specs=out_specs,
        out_shape=out_shape,
        scratch_shapes=[pltpu.VMEM((CTX_LEN + SEQ, HEAD_DIM), BF16),
                        pltpu.VMEM((CTX_LEN + SEQ, 2 * HEAD_DIM), BF16)],
        compiler_params=_params(("arbitrary", "arbitrary", "arbitrary")),
        name="gqa_mixer",
    )(*args)
    return (res[0], res[1]) if with_ctx else (res[0], None)


def _log_sigmoid(v):
    return -(jnp.maximum(-v, 0.0) + jnp.log1p(jnp.exp(-jnp.abs(v))))


RET_HEADS = 2
RET_UNROLL = 4


def _ret_kernel(*refs, with_ctx):
    if with_ctx:
        (ql_ref, kl_ref, vl_ref, gl_ref, qc_ref, kc_ref, vc_ref, gc_ref,
         cos_ref, sa_ref, sb_ref, decf_ref, decb_ref, gng_ref, gnb_ref,
         o_ref, oc_ref, acc_sc, of_sc, ob_sc, qf_sc, qb_sc, kv_sc, cst_sc) = refs
    else:
        (ql_ref, kl_ref, vl_ref, gl_ref, kc_ref, vc_ref,
         cos_ref, sa_ref, sb_ref, decf_ref, decb_ref, gng_ref, gnb_ref,
         o_ref, acc_sc, of_sc, ob_sc, qf_sc, qb_sc, kv_sc, cst_sc) = refs
        qc_ref = None

    n_ctx = CTX_LEN // CHUNK
    n_lat = SEQ // CHUNK
    n_all = n_ctx + n_lat
    ii = lax.broadcasted_iota(jnp.int32, (CHUNK, CHUNK), 0).astype(F32)
    jj = lax.broadcasted_iota(jnp.int32, (CHUNK, CHUNK), 1).astype(F32)
    dfw = ii - jj
    jrow = jj[0:1, :]
    cdec = []
    kdec_rows = []
    for hh in range(RET_HEADS):
        lg_f = _log_sigmoid(decf_ref[hh])
        lg_b = _log_sigmoid(decb_ref[hh])
        base = 4 * hh
        cst_sc[base + 0] = jnp.where(dfw >= 0, jnp.exp(jnp.maximum(dfw, 0.0) * lg_f), 0.0)
        cst_sc[base + 1] = jnp.where(dfw <= 0, jnp.exp(jnp.maximum(-dfw, 0.0) * lg_b), 0.0)
        cst_sc[base + 2] = jnp.exp((ii + 1.0) * lg_f)
        cst_sc[base + 3] = jnp.exp((CHUNK - ii) * lg_b)
        kdec_rows += [jnp.exp((CHUNK - 1.0 - jrow) * lg_f), jnp.exp(jrow * lg_b)]
        cdec += [jnp.exp(CHUNK * lg_f), jnp.exp(CHUNK * lg_b)]

    def intra(c, lat, need_out):
        rows = pl.ds(pl.multiple_of(c * CHUNK, CHUNK), CHUNK)
        src = pl.ds(pl.multiple_of((c - n_ctx) * CHUNK, CHUNK), CHUNK) if lat else rows
        if lat:
            cos, sa, sb = cos_ref[src, :], sa_ref[src, :], sb_ref[src, :]
        for hh in range(RET_HEADS):
            lanes = slice(hh * HEAD_DIM, (hh + 1) * HEAD_DIM)
            base = 4 * hh
            if lat:
                kc = _rope(kl_ref[src, lanes], cos, sa, sb) * ATTN_SCALE
                vc = vl_ref[src, lanes].astype(BF16)
            else:
                kc = kc_ref[src, lanes] * ATTN_SCALE
                vc = vc_ref[src, lanes].astype(BF16)
            if need_out:
                qc = _rope(ql_ref[src, lanes], cos, sa, sb) if lat else qc_ref[src, lanes]
                a = _dot_nt(qc.astype(BF16), kc.astype(BF16))
                att = jnp.concatenate([a * cst_sc[base], a * cst_sc[base + 1]], axis=0)
                o2 = jnp.dot(att.astype(BF16), vc, preferred_element_type=F32)
                acc_sc[hh, rows, :] = o2[:CHUNK] + o2[CHUNK:]
                qf_sc[hh, rows, :] = (qc * cst_sc[base + 2]).astype(BF16)
                qb_sc[hh, rows, :] = (qc * cst_sc[base + 3]).astype(BF16)
            kt = jnp.transpose(kc)
            kd2 = jnp.concatenate([kt * kdec_rows[2 * hh], kt * kdec_rows[2 * hh + 1]], axis=0)
            kv_sc[hh, c] = jnp.dot(kd2.astype(BF16), vc, preferred_element_type=F32)

    for c in range(n_ctx):
        intra(c, False, with_ctx)

    def intra_lat(c, carry):
        intra(c, True, True)
        return carry

    lax.fori_loop(n_ctx, n_all, intra_lat, 0, unroll=RET_UNROLL)

    def inter(hh, direction, c, s, q_ref_, out_ref, need_out):
        if need_out:
            rows = pl.ds(pl.multiple_of(c * CHUNK, CHUNK), CHUNK)
            out_ref[hh, rows, :] = jnp.dot(q_ref_[hh, rows, :], s.astype(BF16),
                                           preferred_element_type=F32)
        kv = kv_sc[hh, c, direction * CHUNK:(direction + 1) * CHUNK, :]
        return s * cdec[2 * hh + direction] + kv

    states = []
    for hh in range(RET_HEADS):
        sf = jnp.zeros((HEAD_DIM, HEAD_DIM), F32)
        sb = jnp.zeros((HEAD_DIM, HEAD_DIM), F32)
        for c in range(n_ctx):
            sf = inter(hh, 0, c, sf, qf_sc, of_sc, with_ctx)
            sb = inter(hh, 1, n_ctx - 1 - c, sb, qb_sc, ob_sc, with_ctx)
        states += [sf, sb]

    def lat_step(t, carry):
        new = []
        for hh in range(RET_HEADS):
            new.append(inter(hh, 0, n_ctx + t, carry[2 * hh], qf_sc, of_sc, True))
            new.append(inter(hh, 1, n_all - 1 - t, carry[2 * hh + 1], qb_sc, ob_sc, True))
        return tuple(new)

    lax.fori_loop(0, n_lat, lat_step, tuple(states), unroll=RET_UNROLL)

    def group_norm(o, gate, lanes):
        mu = jnp.mean(o, axis=-1, keepdims=True)
        var = jnp.mean(jnp.square(o - mu), axis=-1, keepdims=True)
        on = (o - mu) * lax.rsqrt(var + LN_EPS)
        return (on * gng_ref[:, lanes] + gnb_ref[:, lanes]) * _silu(gate)

    for hh in range(RET_HEADS):
        lanes = slice(hh * HEAD_DIM, (hh + 1) * HEAD_DIM)
        o = acc_sc[hh, CTX_LEN:, :] + of_sc[hh, CTX_LEN:, :] + ob_sc[hh, CTX_LEN:, :]
        o_ref[:, lanes] = group_norm(o, gl_ref[:, lanes], lanes).astype(o_ref.dtype)
        if with_ctx:
            oc = acc_sc[hh, 0:CTX_LEN, :] + of_sc[hh, 0:CTX_LEN, :] + ob_sc[hh, 0:CTX_LEN, :]
            oc_ref[:, lanes] = group_norm(oc, gc_ref[:, lanes], lanes).astype(oc_ref.dtype)


def _retention_mixer(p_lat, p_ctx, ctx_cols, rope_tabs, dec_f, dec_b, gn_g3, gn_b3, l, with_ctx):
    cos, sin_a, sin_b = rope_tabs
    kern = functools.partial(_ret_kernel, with_ctx=with_ctx)
    width = RET_HEADS * HEAD_DIM

    def col(c0, rows):
        return pl.BlockSpec((rows, width), lambda b, h: (b, c0 // RET_HEADS + h))

    in_specs = [col(COL_C_Q, SEQ), col(COL_C_K, SEQ), col(COL_C_V, SEQ), col(COL_C_GATE, SEQ)]
    args = [p_lat, p_lat, p_lat, p_lat]
    if with_ctx:
        in_specs += [col(COL_C_Q, CTX_LEN), col(COL_C_K, CTX_LEN), col(COL_C_V, CTX_LEN),
                     col(COL_C_GATE, CTX_LEN)]
        args += [p_ctx, p_ctx, p_ctx, p_ctx]
    else:
        in_specs += [col(ctx_cols["c_k"], CTX_LEN), col(ctx_cols["c_v"], CTX_LEN)]
        args += [p_ctx, p_ctx]
    tab = pl.BlockSpec((SEQ, HEAD_DIM), lambda b, h: (0, 0))
    dec = pl.BlockSpec((None, RET_HEADS, 1, HEAD_DIM), lambda b, h: (l, h, 0, 0))
    gvec = pl.BlockSpec((None, 1, width), lambda b, h: (l, 0, h))
    in_specs += [tab, tab, tab, dec, dec, gvec, gvec]
    args += [cos, sin_a, sin_b, dec_f, dec_b, gn_g3, gn_b3]
    out_specs = [pl.BlockSpec((SEQ, width), lambda b, h: (b, h))]
    out_shape = [jax.ShapeDtypeStruct((BATCH * SEQ, GROUP_W), BF16)]
    if with_ctx:
        out_specs.append(pl.BlockSpec((CTX_LEN, width), lambda b, h: (b, h)))
        out_shape.append(jax.ShapeDtypeStruct((BATCH * CTX_LEN, GROUP_W), BF16))
    n_rows = CTX_LEN + SEQ
    rows3 = (RET_HEADS, n_rows, HEAD_DIM)
    res = pl.pallas_call(
        kern,
        grid=(BATCH, N_HEADS // RET_HEADS),
        in_specs=in_specs,
        out_specs=out_specs,
        out_shape=out_shape,
        scratch_shapes=[pltpu.VMEM(rows3, F32), pltpu.VMEM(rows3, F32), pltpu.VMEM(rows3, F32),
                        pltpu.VMEM(rows3, BF16), pltpu.VMEM(rows3, BF16),
                        pltpu.VMEM((RET_HEADS, n_rows // CHUNK, 2 * CHUNK, HEAD_DIM), F32),
                        pltpu.VMEM((RET_HEADS * 4, CHUNK, CHUNK), F32)],
        compiler_params=_params(("arbitrary", "arbitrary")),
        name="retention_mixer",
    )(*args)
    return (res[0], res[1]) if with_ctx else (res[0], None)


def _na_row_start(r):
    return min(max(r - NA_WIN_R // 2, 0), GRID_H - NA_WIN_R)


def _na_key_start(blk):
    return min(max(blk * NA_BLK_ROWS - NA_WIN_R // 2, 0), GRID_H - NA_BLK_KROWS)


def _na_table_kernel(tab_ref, bias_ref, mask_ref):
    q = lax.broadcasted_iota(jnp.int32, (GRID_W, HEAD_DIM), 0)
    lane = lax.broadcasted_iota(jnp.int32, (GRID_W, HEAD_DIM), 1)
    kcol = lane % GRID_W
    lower_half = lane < GRID_W
    off_c = jnp.clip(kcol - q, -(NA_WIN_C - 1), NA_WIN_C - 1) + (NA_WIN_C - 1)
    c_start = jnp.clip(q - NA_WIN_C // 2, 0, GRID_W - NA_WIN_C)
    col_win = jnp.where((kcol >= c_start) & (kcol < c_start + NA_WIN_C), 1.0, 0.0)
    zeros = jnp.zeros((GRID_W, HEAD_DIM), F32)
    planes = []
    for o in range(2 * NA_WIN_R - 1):
        g = zeros
        for c in range(2 * NA_WIN_C - 1):
            g = jnp.where(off_c == c, tab_ref[0, o, c] * LOG2E, g)
        planes.append(g)
    n_blk = GRID_H // NA_BLK_ROWS
    for pat, blk in enumerate((0, 1, n_blk - 1)):
        k0 = _na_key_start(blk)
        for rq in range(NA_BLK_ROWS):
            r = blk * NA_BLK_ROWS + rq
            rs = _na_row_start(r)
            for jp in range(NA_BLK_KROWS // 2):
                b_half, m_half = [], []
                for key_row in (k0 + 2 * jp, k0 + 2 * jp + 1):
                    valid = rs <= key_row < rs + NA_WIN_R
                    b_half.append(planes[key_row - r + NA_WIN_R - 1] if valid else zeros)
                    m_half.append(col_win if valid else zeros)
                rows = slice(rq * GRID_W, (rq + 1) * GRID_W)
                lanes = slice(jp * HEAD_DIM, (jp + 1) * HEAD_DIM)
                bias_ref[0, pat, rows, lanes] = jnp.where(lower_half, b_half[0], b_half[1])
                mask_ref[0, pat, rows, lanes] = jnp.where(lower_half, m_half[0], m_half[1])


def _na_tables(na_bias_l):
    spec = pl.BlockSpec((1, NA_PATTERNS, NA_BLK_Q, NA_BLK_KEYS), lambda h: (h, 0, 0, 0))
    shape = jax.ShapeDtypeStruct((N_HEADS, NA_PATTERNS, NA_BLK_Q, NA_BLK_KEYS), F32)
    return pl.pallas_call(
        _na_table_kernel,
        grid=(N_HEADS,),
        in_specs=[pl.BlockSpec((1, 2 * NA_WIN_R - 1, 2 * NA_WIN_C - 1), lambda h: (h, 0, 0),
                               memory_space=pltpu.SMEM)],
        out_specs=[spec, spec],
        out_shape=[shape, shape],
        compiler_params=_params(("arbitrary",)),
        name="na_tables",
    )(na_bias_l)


def _na_kernel(*refs, with_ctx):
    if with_ctx:
        (q_ref, k_ref, v_ref, g_ref, qc_ref, kc_ref, vc_ref, gc_ref, bias_ref, mask_ref,
         o_ref, oc_ref, q_sc, k_sc, v_sc, scx_sc) = refs
    else:
        (q_ref, k_ref, v_ref, g_ref, kc_ref, vc_ref, bias_ref, mask_ref,
         o_ref, q_sc, k_sc, v_sc, scx_sc) = refs
    scale = ATTN_SCALE * LOG2E
    q_sc[...] = (q_ref[...] * scale).astype(BF16)
    k_sc[...] = k_ref[...].astype(BF16)
    v_sc[...] = v_ref[...].astype(BF16)
    kc = kc_ref[...].astype(BF16)
    vc = vc_ref[...].astype(BF16)
    scx_sc[...] = _dot_nt(q_sc[...], kc)
    n_blk = GRID_H // NA_BLK_ROWS

    def blk_step(blk, carry):
        k0 = jnp.clip(blk * NA_BLK_ROWS - NA_WIN_R // 2, 0, GRID_H - NA_BLK_KROWS)
        pat = jnp.where(blk == 0, 0, jnp.where(blk == n_blk - 1, 2, 1))
        rows = pl.ds(pl.multiple_of(blk * NA_BLK_Q, NA_BLK_Q), NA_BLK_Q)
        keys = pl.ds(pl.multiple_of(k0 * GRID_W, GRID_W), NA_BLK_KEYS)
        s_nb = _dot_nt(q_sc[rows, :], k_sc[keys, :]) + bias_ref[0, pat]
        s_nb = jnp.where(mask_ref[0, pat] > 0.5, s_nb, MASK_VALUE)
        o = _softmax_pv([s_nb, scx_sc[rows, :]], [v_sc[keys, :], vc])
        o_ref[rows, :] = (o * _silu(g_ref[rows, :])).astype(o_ref.dtype)
        return carry

    lax.fori_loop(0, n_blk, blk_step, 0, unroll=NA_UNROLL)

    if with_ctx:
        s = _dot_nt((qc_ref[...] * scale).astype(BF16), kc)
        o = _softmax_pv([s], [vc])
        oc_ref[...] = (o * _silu(gc_ref[...])).astype(oc_ref.dtype)


def _na_mixer(p_lat, p_ctx, ctx_cols, na_tabs, with_ctx):
    kern = functools.partial(_na_kernel, with_ctx=with_ctx)

    def col(c0, rows):
        return pl.BlockSpec((rows, HEAD_DIM), lambda b, h: (b, c0 + h))

    in_specs = [col(COL_D_Q, SEQ), col(COL_D_K, SEQ), col(COL_D_V, SEQ), col(COL_D_GATE, SEQ)]
    args = [p_lat, p_lat, p_lat, p_lat]
    if with_ctx:
        in_specs += [col(COL_D_Q, CTX_LEN), col(COL_D_K, CTX_LEN), col(COL_D_V, CTX_LEN),
                     col(COL_D_GATE, CTX_LEN)]
        args += [p_ctx, p_ctx, p_ctx, p_ctx]
    else:
        in_specs += [col(ctx_cols["d_k"], CTX_LEN), col(ctx_cols["d_v"], CTX_LEN)]
        args += [p_ctx, p_ctx]
    tab = pl.BlockSpec((1, NA_PATTERNS, NA_BLK_Q, NA_BLK_KEYS), lambda b, h: (h, 0, 0, 0))
    in_specs += [tab, tab]
    args += list(na_tabs)
    out_specs = [pl.BlockSpec((SEQ, HEAD_DIM), lambda b, h: (b, h))]
    out_shape = [jax.ShapeDtypeStruct((BATCH * SEQ, GROUP_W), BF16)]
    if with_ctx:
        out_specs.append(pl.BlockSpec((CTX_LEN, HEAD_DIM), lambda b, h: (b, h)))
        out_shape.append(jax.ShapeDtypeStruct((BATCH * CTX_LEN, GROUP_W), BF16))
    res = pl.pallas_call(
        kern,
        grid=(BATCH, N_HEADS),
        in_specs=in_specs,
        out_specs=out_specs,
        out_shape=out_shape,
        scratch_shapes=[pltpu.VMEM((SEQ, HEAD_DIM), BF16), pltpu.VMEM((SEQ, HEAD_DIM), BF16),
                        pltpu.VMEM((SEQ, HEAD_DIM), BF16), pltpu.VMEM((SEQ, CTX_LEN), F32)],
        compiler_params=_params(("arbitrary", "arbitrary")),
        name="na_mixer",
    )(*args)
    return (res[0], res[1]) if with_ctx else (res[0], None)


def _rope_tables():
    t = jnp.arange(SEQ)
    row = (t // GRID_W).astype(F32)
    col = (t % GRID_W).astype(F32)
    half = HEAD_DIM // 2
    inv_freq = ROPE_THETA ** (-jnp.arange(0, half, 2, dtype=F32) / half)
    ang_r = row[:, None] * inv_freq[None, :]
    ang_c = col[:, None] * inv_freq[None, :]
    ang = jnp.concatenate([ang_r, ang_r, ang_c, ang_c], axis=-1)
    cos, sin = jnp.cos(ang), jnp.sin(ang)
    quarter = (jnp.arange(HEAD_DIM) // (HEAD_DIM // 4)) % 2
    sin_a = jnp.where(quarter[None, :] == 0, -sin, 0.0)
    sin_b = jnp.where(quarter[None, :] == 1, sin, 0.0)
    return cos, sin_a, sin_b


def kernel(x, c, ctx, c_ctx, ada_w, ada_b, norm_g, w_in, conv_w, conv_b, conv_ln_g, conv_ln_b,
           gqa_qn_g, gqa_kn_g, ret_decay_fwd, ret_decay_bwd, ret_gn_g, ret_gn_b, na_bias,
           w_out, final_g):
    rope_tabs = _rope_tables()
    x_lat = x.reshape(BATCH * SEQ, D_MODEL)
    x_ctx = ctx.reshape(BATCH * CTX_LEN, D_MODEL)
    c8 = jnp.concatenate(
        [c, c_ctx[None, :], jnp.zeros((MOD_ROWS - BATCH - 1, D_MODEL), F32)], axis=0)

    def vec3(a):
        return a.reshape(DEPTH, 1, a.shape[-1])

    ada_b3, norm_g3 = vec3(ada_b), vec3(norm_g)
    conv_b3, ln_g3, ln_b3 = vec3(conv_b), vec3(conv_ln_g), vec3(conv_ln_b)
    qn3, kn3 = vec3(gqa_qn_g), vec3(gqa_kn_g)
    gn_g3, gn_b3 = vec3(ret_gn_g), vec3(ret_gn_b)
    dec_f = jnp.broadcast_to(ret_decay_fwd[:, :, None, None], (DEPTH, N_HEADS, 1, HEAD_DIM))
    dec_b = jnp.broadcast_to(ret_decay_bwd[:, :, None, None], (DEPTH, N_HEADS, 1, HEAD_DIM))

    for l in range(DEPTH):
        with_ctx = l < DEPTH - 1
        mod = _ada_table(c8, ada_w, ada_b3, l)
        h_lat = _norm_modulate(x_lat, norm_g3, l, mod, SEQ, None)
        h_ctx = _norm_modulate(x_ctx, norm_g3, l, mod, None, MOD_CTX_ROW)
        p_lat = _in_proj(h_lat, w_in, l)
        p_ctx = _in_proj(h_ctx, w_in, l, kv_only=not with_ctx)
        ctx_cols = CTX_COLS_FULL if with_ctx else CTX_COLS_KV_ONLY

        ya_l = _conv_mixer(p_lat, SEQ, conv_w, conv_b3, ln_g3, ln_b3, l)
        yb_l, yb_c = _gqa_mixer(p_lat, p_ctx, ctx_cols, rope_tabs, qn3, kn3, l, with_ctx)
        yc_l, yc_c = _retention_mixer(p_lat, p_ctx, ctx_cols, rope_tabs, dec_f, dec_b, gn_g3, gn_b3,
                                      l, with_ctx)
        yd_l, yd_c = _na_mixer(p_lat, p_ctx, ctx_cols, _na_tables(na_bias[l]), with_ctx)

        x_new = _out_proj((ya_l, yb_l, yc_l, yd_l), w_out, l, x_lat, mod, SEQ, None)
        if with_ctx:
            ya_c = _conv_mixer(p_ctx, CTX_LEN, conv_w, conv_b3, ln_g3, ln_b3, l)
            x_ctx = _out_proj((ya_c, yb_c, yc_c, yd_c), w_out, l, x_ctx, mod, None, MOD_CTX_ROW)
        x_lat = x_new

    out = _final_norm(x_lat, final_g.reshape(1, D_MODEL))
    return out.reshape(BATCH, SEQ, D_MODEL)
```

```python
import functools

import jax
import jax.numpy as jnp
from jax import lax
from jax.experimental import pallas as pl
from jax.experimental.pallas import tpu as pltpu

F32 = jnp.float32
BF16 = jnp.bfloat16

D_MODEL = 4096
BATCH = 4
SEQ = 2048
DEPTH = 2
GRID_W = 64
GRID_H = SEQ // GRID_W
CTX_LEN = 256
GROUP_W = 1024
HEAD_DIM = 128
N_HEADS = GROUP_W // HEAD_DIM
KV_HEADS = 2
Q_PER_KV = N_HEADS // KV_HEADS
CONV_WIDTH = 31
CONV_HALF = CONV_WIDTH // 2
CHUNK = 128
NA_WIN_R = 8
NA_WIN_C = 16
ROPE_THETA = 10000.0
NORM_EPS = 1e-6
LN_EPS = 1e-5
ATTN_SCALE = HEAD_DIM ** -0.5
MASK_VALUE = -1e30
P_TOTAL = 13 * GROUP_W + 2 * KV_HEADS * HEAD_DIM

COL_A_GLU = 0
COL_A_GATE = 16
COL_B_Q = 24
COL_B_K = 32
COL_B_V = 34
COL_B_GATE = 36
COL_C_Q = 44
COL_C_K = 52
COL_C_V = 60
COL_C_GATE = 68
COL_D_Q = 76
COL_D_K = 84
COL_D_V = 92
COL_D_GATE = 100

NORM_TM = 512
MOD_ROWS = 8
MOD_CTX_ROW = BATCH
VMEM_LIMIT = 56 * 1024 * 1024

NA_BLK_ROWS = 4
NA_BLK_KROWS = 12
NA_BLK_Q = NA_BLK_ROWS * GRID_W
NA_BLK_KEYS = NA_BLK_KROWS * GRID_W
NA_PATTERNS = 3
NA_UNROLL = 4
NA_HEADS = 2


def _params(sem, flags=None):
    return pltpu.CompilerParams(dimension_semantics=sem, vmem_limit_bytes=VMEM_LIMIT, flags=flags)


def _silu(v):
    return v * jax.nn.sigmoid(v)


def _mod_kernel(c_ref, w_ref, b_ref, o_ref):
    a = _silu(c_ref[...]).astype(BF16)
    o_ref[...] = jnp.dot(a, w_ref[...].astype(BF16), preferred_element_type=F32) + b_ref[...]


def _ada_table(c8, ada_w, ada_b3, l):
    tn = 512
    n = 3 * D_MODEL
    return pl.pallas_call(
        _mod_kernel,
        grid=(n // tn,),
        in_specs=[
            pl.BlockSpec((MOD_ROWS, D_MODEL), lambda j: (0, 0)),
            pl.BlockSpec((None, D_MODEL, tn), lambda j: (l, 0, j)),
            pl.BlockSpec((None, 1, tn), lambda j: (l, 0, j)),
        ],
        out_specs=pl.BlockSpec((MOD_ROWS, tn), lambda j: (0, j)),
        out_shape=jax.ShapeDtypeStruct((MOD_ROWS, n), F32),
        compiler_params=_params(("arbitrary",)),
        name="ada_table",
    )(c8, ada_w, ada_b3)


def _normmod_kernel(x_ref, g_ref, shift_ref, scale_ref, o_ref, *, tiles_per_row, fixed_row):
    if fixed_row is None:
        r = pl.program_id(0) // tiles_per_row
    else:
        r = fixed_row
    x = x_ref[...]
    y = x * lax.rsqrt(jnp.mean(x * x, axis=-1, keepdims=True) + NORM_EPS)
    y = y * g_ref[...]
    shift = shift_ref[pl.ds(r, 1), :]
    scale = scale_ref[pl.ds(r, 1), :]
    o_ref[...] = (y * (1.0 + scale) + shift).astype(o_ref.dtype)


def _norm_modulate(x2, norm_g3, l, mod, rows_per_mod, fixed_row):
    m = x2.shape[0]
    tm = NORM_TM
    tiles_per_row = None if rows_per_mod is None else rows_per_mod // tm
    kern = functools.partial(_normmod_kernel, tiles_per_row=tiles_per_row, fixed_row=fixed_row)
    return pl.pallas_call(
        kern,
        grid=(m // tm,),
        in_specs=[
            pl.BlockSpec((tm, D_MODEL), lambda i: (i, 0)),
            pl.BlockSpec((None, 1, D_MODEL), lambda i: (l, 0, 0)),
            pl.BlockSpec((MOD_ROWS, D_MODEL), lambda i: (0, 0)),
            pl.BlockSpec((MOD_ROWS, D_MODEL), lambda i: (0, 1)),
        ],
        out_specs=pl.BlockSpec((tm, D_MODEL), lambda i: (i, 0)),
        out_shape=jax.ShapeDtypeStruct((m, D_MODEL), BF16),
        compiler_params=_params(("arbitrary",)),
        name="norm_modulate",
    )(x2, norm_g3, mod, mod)


PROJ_TM = 1024
PROJ_TN = 512


def _stage_weight_chunk(w_ref, wbf_ref, n_cols):
    j, i = pl.program_id(0), pl.program_id(1)
    chunk = w_ref.shape[0]

    @pl.when(j < n_cols)
    def _():
        wbf_ref[j % 2, pl.ds(pl.multiple_of(i * chunk, chunk), chunk), :] = w_ref[...].astype(BF16)

    return j, (j + 1) % 2


def _mm_kernel(a_ref, w_ref, o_ref, wbf_ref, *, n_cols):
    j, slot = _stage_weight_chunk(w_ref, wbf_ref, n_cols)

    @pl.when(j > 0)
    def _():
        o_ref[...] = jnp.dot(a_ref[...], wbf_ref[slot], preferred_element_type=F32)


def _ctx_kv_col_tile(j):
    b_kv = COL_B_K * HEAD_DIM // PROJ_TN
    c_kv = COL_C_K * HEAD_DIM // PROJ_TN
    d_kv = COL_D_K * HEAD_DIM // PROJ_TN
    n_c = 2 * GROUP_W // PROJ_TN
    return jnp.where(j < 1, b_kv, jnp.where(j < 1 + n_c, c_kv + (j - 1), d_kv + (j - 1 - n_c)))


CTX_KV_TILES = 1 + 2 * (2 * GROUP_W // PROJ_TN)
CTX_COLS_FULL = dict(b_k=COL_B_K, b_v=COL_B_V, c_k=COL_C_K, c_v=COL_C_V, d_k=COL_D_K, d_v=COL_D_V)
CTX_COLS_KV_ONLY = dict(b_k=0, b_v=KV_HEADS, c_k=2 * KV_HEADS, c_v=2 * KV_HEADS + N_HEADS,
                        d_k=2 * KV_HEADS + 2 * N_HEADS, d_v=2 * KV_HEADS + 3 * N_HEADS)


def _in_proj(h, w_in, l, kv_only=False):
    m = h.shape[0]
    tm, tn = PROJ_TM, PROJ_TN
    n_tiles = CTX_KV_TILES if kv_only else P_TOTAL // tn
    col = _ctx_kv_col_tile if kv_only else (lambda j: j)
    n_rows = m // tm
    chunk = D_MODEL // n_rows
    return pl.pallas_call(
        functools.partial(_mm_kernel, n_cols=n_tiles),
        grid=(n_tiles + 1, n_rows),
        in_specs=[
            pl.BlockSpec((tm, D_MODEL), lambda j, i: (jnp.where(j == 0, 0, i), 0)),
            pl.BlockSpec((None, chunk, tn), lambda j, i: (l, i, col(jnp.minimum(j, n_tiles - 1)))),
        ],
        out_specs=pl.BlockSpec((tm, tn), lambda j, i: (jnp.where(j == 0, 0, i), jnp.maximum(j - 1, 0))),
        out_shape=jax.ShapeDtypeStruct((m, n_tiles * tn), F32),
        scratch_shapes=[pltpu.VMEM((2, D_MODEL, tn), BF16)],
        compiler_params=_params(("arbitrary", "arbitrary")),
        name="in_proj",
    )(h, w_in)


def _out_kernel(ya_ref, yb_ref, yc_ref, yd_ref, w_ref, x_ref, gate_ref, o_ref, wbf_ref, *,
                tiles_per_row, fixed_row, n_cols):
    if fixed_row is None:
        r = pl.program_id(1) // tiles_per_row
    else:
        r = fixed_row
    j, slot = _stage_weight_chunk(w_ref, wbf_ref, n_cols)

    @pl.when(j > 0)
    def _():
        acc = None
        for g, y_ref in enumerate((ya_ref, yb_ref, yc_ref, yd_ref)):
            part = jnp.dot(y_ref[...], wbf_ref[slot, g * GROUP_W:(g + 1) * GROUP_W, :],
                           preferred_element_type=F32)
            acc = part if acc is None else acc + part
        gate = gate_ref[pl.ds(r, 1), :]
        o_ref[...] = x_ref[...] + gate * acc


def _out_proj(ys, w_out, l, x2, mod, rows_per_mod, fixed_row):
    m = x2.shape[0]
    tm, tn = PROJ_TM, PROJ_TN
    tiles_per_row = None if rows_per_mod is None else rows_per_mod // tm
    n_cols = D_MODEL // tn
    n_rows = m // tm
    chunk = D_MODEL // n_rows
    kern = functools.partial(_out_kernel, tiles_per_row=tiles_per_row, fixed_row=fixed_row,
                             n_cols=n_cols)

    def out_col(j):
        return jnp.maximum(j - 1, 0)

    def row(j, i):
        return jnp.where(j == 0, 0, i)

    y_spec = pl.BlockSpec((tm, GROUP_W), lambda j, i: (row(j, i), 0))
    gate_col0 = 2 * D_MODEL // tn
    return pl.pallas_call(
        kern,
        grid=(n_cols + 1, n_rows),
        in_specs=[
            y_spec, y_spec, y_spec, y_spec,
            pl.BlockSpec((None, chunk, tn), lambda j, i: (l, i, jnp.minimum(j, n_cols - 1))),
            pl.BlockSpec((tm, tn), lambda j, i: (row(j, i), out_col(j))),
            pl.BlockSpec((MOD_ROWS, tn), lambda j, i: (0, gate_col0 + out_col(j))),
        ],
        out_specs=pl.BlockSpec((tm, tn), lambda j, i: (row(j, i), out_col(j))),
        out_shape=jax.ShapeDtypeStruct((m, D_MODEL), F32),
        scratch_shapes=[pltpu.VMEM((2, D_MODEL, tn), BF16)],
        compiler_params=_params(("arbitrary", "arbitrary")),
        name="out_proj",
    )(*ys, w_out, x2, mod)


def _final_norm_kernel(x_ref, g_ref, o_ref):
    x = x_ref[...]
    o_ref[...] = x * lax.rsqrt(jnp.mean(x * x, axis=-1, keepdims=True) + NORM_EPS) * g_ref[...]


def _final_norm(x2, g2):
    m = x2.shape[0]
    tm = NORM_TM
    return pl.pallas_call(
        _final_norm_kernel,
        grid=(m // tm,),
        in_specs=[pl.BlockSpec((tm, D_MODEL), lambda i: (i, 0)),
                  pl.BlockSpec((1, D_MODEL), lambda i: (0, 0))],
        out_specs=pl.BlockSpec((tm, D_MODEL), lambda i: (i, 0)),
        out_shape=jax.ShapeDtypeStruct((m, D_MODEL), F32),
        compiler_params=_params(("arbitrary",)),
        name="final_norm",
    )(x2, g2)


CONV_HALO = 16
CONV_SUB = 64
SUBLANES = 8


def _conv_kernel(cur_ref, prev_ref, next_ref, gate_ref, w_ref, b_ref, lng_ref, lnb_ref, o_ref,
                 u_ref, ush_ref, c_ref, *, tile, n_tiles):
    t = pl.program_id(1)

    def glu(ref):
        return ref[:, :GROUP_W] * jax.nn.sigmoid(ref[:, GROUP_W:])

    u_ref[CONV_HALO:CONV_HALO + tile, :] = glu(cur_ref)
    u_ref[0:CONV_HALO, :] = jnp.where(t > 0, glu(prev_ref), 0.0)
    u_ref[CONV_HALO + tile:, :] = jnp.where(t < n_tiles - 1, glu(next_ref), 0.0)

    sh_rows = ush_ref.shape[1]
    for b in range(1, SUBLANES):
        ush_ref[b - 1] = u_ref[b:b + sh_rows, :]

    base = CONV_HALO - CONV_HALF

    for s in range(tile // CONV_SUB):
        row0 = s * CONV_SUB
        for c in range(GROUP_W // HEAD_DIM):
            lanes = slice(c * HEAD_DIM, (c + 1) * HEAD_DIM)
            acc = jnp.zeros((CONV_SUB, HEAD_DIM), F32)
            for k in range(CONV_WIDTH):
                b = (base + k) % SUBLANES
                rows = slice(row0 + base + k - b, row0 + base + k - b + CONV_SUB)
                tap = u_ref[rows, lanes] if b == 0 else ush_ref[b - 1, rows, lanes]
                acc = acc + w_ref[k:k + 1, lanes] * tap
            c_ref[row0:row0 + CONV_SUB, lanes] = acc + b_ref[:, lanes]

    v = c_ref[...]
    mu = jnp.mean(v, axis=-1, keepdims=True)
    var = jnp.mean(jnp.square(v - mu), axis=-1, keepdims=True)
    y = (v - mu) * lax.rsqrt(var + LN_EPS) * lng_ref[...] + lnb_ref[...]
    o_ref[...] = (_silu(y) * _silu(gate_ref[...])).astype(o_ref.dtype)


def _conv_mixer(p, seq_len, conv_w, conv_b3, ln_g3, ln_b3, l):
    tile = 256
    n_tiles = seq_len // tile
    n_seq = p.shape[0] // seq_len
    halo_per_tile = tile // CONV_HALO
    last_halo = p.shape[0] // CONV_HALO - 1
    kern = functools.partial(_conv_kernel, tile=tile, n_tiles=n_tiles)
    vec = pl.BlockSpec((None, 1, GROUP_W), lambda b, t: (l, 0, 0))
    return pl.pallas_call(
        kern,
        grid=(n_seq, n_tiles),
        in_specs=[
            pl.BlockSpec((tile, 2 * GROUP_W), lambda b, t: (b * n_tiles + t, 0)),
            pl.BlockSpec((CONV_HALO, 2 * GROUP_W),
                         lambda b, t: (jnp.maximum((b * n_tiles + t) * halo_per_tile - 1, 0), 0)),
            pl.BlockSpec((CONV_HALO, 2 * GROUP_W),
                         lambda b, t: (jnp.minimum((b * n_tiles + t + 1) * halo_per_tile, last_halo), 0)),
            pl.BlockSpec((tile, GROUP_W), lambda b, t: (b * n_tiles + t, COL_A_GATE // 8)),
            pl.BlockSpec((None, CONV_WIDTH, GROUP_W), lambda b, t: (l, 0, 0)),
            vec, vec, vec,
        ],
        out_specs=pl.BlockSpec((tile, GROUP_W), lambda b, t: (b * n_tiles + t, 0)),
        out_shape=jax.ShapeDtypeStruct((p.shape[0], GROUP_W), BF16),
        scratch_shapes=[pltpu.VMEM((tile + 2 * CONV_HALO, GROUP_W), F32),
                        pltpu.VMEM((SUBLANES - 1, tile + 2 * CONV_HALO - SUBLANES, GROUP_W), F32),
                        pltpu.VMEM((tile, GROUP_W), F32)],
        compiler_params=_params(("arbitrary", "arbitrary")),
        name="conv_mixer",
    )(p, p, p, p, conv_w, conv_b3, ln_g3, ln_b3)


def _head_rmsnorm(v, g):
    return v * lax.rsqrt(jnp.mean(v * v, axis=-1, keepdims=True) + NORM_EPS) * g


def _rope(v, cos, sin_a, sin_b):
    return (v * cos + pltpu.roll(v, HEAD_DIM - HEAD_DIM // 4, 1) * sin_a
            + pltpu.roll(v, HEAD_DIM // 4, 1) * sin_b)


def _dot_nt(a, b):
    return lax.dot_general(a, b, (((1,), (1,)), ((), ())), preferred_element_type=F32)


LOG2E = 1.4426950408889634


def _softmax_pv(s_list, v_list):
    m = s_list[0].max(axis=-1, keepdims=True)
    for s in s_list[1:]:
        m = jnp.maximum(m, s.max(axis=-1, keepdims=True))
    den = None
    acc = None
    for s, v in zip(s_list, v_list):
        e = jnp.exp2(s - m)
        d = e.sum(axis=-1, keepdims=True)
        o = jnp.dot(e.astype(BF16), v, preferred_element_type=F32)
        den = d if den is None else den + d
        acc = o if acc is None else acc + o
    return acc / den


def _gqa_kernel(*refs, tq, with_ctx):
    if with_ctx:
        (q_ref, kl_ref, vl_ref, kc_ref, vc_ref, g_ref, qc_ref, gc_ref,
         cos_q, sa_q, sb_q, cos_k, sa_k, sb_k, qn_ref, kn_ref,
         o_ref, oc_ref, k_sc, v_sc) = refs
    else:
        (q_ref, kl_ref, vl_ref, kc_ref, vc_ref, g_ref,
         cos_q, sa_q, sb_q, cos_k, sa_k, sb_k, qn_ref, kn_ref,
         o_ref, k_sc, v_sc) = refs
    qi = pl.program_id(2)
    qn = qn_ref[...] * (ATTN_SCALE * LOG2E)
    kn = kn_ref[...]

    @pl.when(qi == 0)
    def _():
        k_sc[0:CTX_LEN, :] = _head_rmsnorm(kc_ref[...], kn).astype(BF16)
        k_sc[CTX_LEN:, :] = _rope(_head_rmsnorm(kl_ref[...], kn),
                                  cos_k[...], sa_k[...], sb_k[...]).astype(BF16)
        v_sc[0:CTX_LEN, 0:HEAD_DIM] = vc_ref[...].astype(BF16)
        v_sc[CTX_LEN:, 0:HEAD_DIM] = vl_ref[...].astype(BF16)
        v_sc[:, HEAD_DIM:] = jnp.ones((CTX_LEN + SEQ, HEAD_DIM), BF16)
        if with_ctx:
            kc = k_sc[0:CTX_LEN, :]
            vc = v_sc[0:CTX_LEN, 0:HEAD_DIM]
            for h in range(Q_PER_KV):
                lanes = slice(h * HEAD_DIM, (h + 1) * HEAD_DIM)
                qh = _head_rmsnorm(qc_ref[:, lanes], qn).astype(BF16)
                o = _softmax_pv([_dot_nt(qh, kc)], [vc])
                oc_ref[:, lanes] = (o * _silu(gc_ref[:, lanes])).astype(oc_ref.dtype)

    k_all = k_sc[...]
    v_all = v_sc[...]
    for h in range(Q_PER_KV):
        lanes = slice(h * HEAD_DIM, (h + 1) * HEAD_DIM)
        for r in range(tq // GQA_ROWS):
            rows = slice(r * GQA_ROWS, (r + 1) * GQA_ROWS)
            qh = _rope(_head_rmsnorm(q_ref[rows, lanes], qn),
                       cos_q[rows, :], sa_q[rows, :], sb_q[rows, :]).astype(BF16)
            s = _dot_nt(qh, k_all)
            e = jnp.exp2(s - s.max(axis=-1, keepdims=True)).astype(BF16)
            o2 = jnp.dot(e, v_all, preferred_element_type=F32)
            o = o2[:, 0:HEAD_DIM] / o2[:, HEAD_DIM:HEAD_DIM + 1]
            o_ref[rows, lanes] = (o * _silu(g_ref[rows, lanes])).astype(o_ref.dtype)


GQA_ROWS = 256
GQA_TQ = 1024


def _gqa_mixer(p_lat, p_ctx, ctx_cols, rope_tabs, qn3, kn3, l, with_ctx):
    tq = GQA_TQ
    ck, cv = ctx_cols["b_k"], ctx_cols["b_v"]
    nq = SEQ // tq
    qw = Q_PER_KV * HEAD_DIM
    cos, sin_a, sin_b = rope_tabs
    kern = functools.partial(_gqa_kernel, tq=tq, with_ctx=with_ctx)
    in_specs = [
        pl.BlockSpec((tq, qw), lambda b, kv, qi: (b * nq + qi, COL_B_Q // Q_PER_KV + kv)),
        pl.BlockSpec((SEQ, HEAD_DIM), lambda b, kv, qi: (b, COL_B_K + kv)),
        pl.BlockSpec((SEQ, HEAD_DIM), lambda b, kv, qi: (b, COL_B_V + kv)),
        pl.BlockSpec((CTX_LEN, HEAD_DIM), lambda b, kv, qi: (b, ck + kv)),
        pl.BlockSpec((CTX_LEN, HEAD_DIM), lambda b, kv, qi: (b, cv + kv)),
        pl.BlockSpec((tq, qw), lambda b, kv, qi: (b * nq + qi, COL_B_GATE // Q_PER_KV + kv)),
    ]
    args = [p_lat, p_lat, p_lat, p_ctx, p_ctx, p_lat]
    if with_ctx:
        in_specs += [
            pl.BlockSpec((CTX_LEN, qw), lambda b, kv, qi: (b, COL_B_Q // Q_PER_KV + kv)),
            pl.BlockSpec((CTX_LEN, qw), lambda b, kv, qi: (b, COL_B_GATE // Q_PER_KV + kv)),
        ]
        args += [p_ctx, p_ctx]
    tab_q = pl.BlockSpec((tq, HEAD_DIM), lambda b, kv, qi: (qi, 0))
    tab_k = pl.BlockSpec((SEQ, HEAD_DIM), lambda b, kv, qi: (0, 0))
    gvec = pl.BlockSpec((None, 1, HEAD_DIM), lambda b, kv, qi: (l, 0, 0))
    in_specs += [tab_q, tab_q, tab_q, tab_k, tab_k, tab_k, gvec, gvec]
    args += [cos, sin_a, sin_b, cos, sin_a, sin_b, qn3, kn3]
    out_specs = [pl.BlockSpec((tq, qw), lambda b, kv, qi: (b * nq + qi, kv))]
    out_shape = [jax.ShapeDtypeStruct((BATCH * SEQ, GROUP_W), BF16)]
    if with_ctx:
        out_specs.append(pl.BlockSpec((CTX_LEN, qw), lambda b, kv, qi: (b, kv)))
        out_shape.append(jax.ShapeDtypeStruct((BATCH * CTX_LEN, GROUP_W), BF16))
    res = pl.pallas_call(
        kern,
        grid=(BATCH, KV_HEADS, nq),
        in_specs=in_specs,
        out_specs=out_specs,
        out_shape=out_shape,
        scratch_shapes=[pltpu.VMEM((CTX_LEN + SEQ, HEAD_DIM), BF16),
                        pltpu.VMEM((CTX_LEN + SEQ, 2 * HEAD_DIM), BF16)],
        compiler_params=_params(("arbitrary", "arbitrary", "arbitrary")),
        name="gqa_mixer",
    )(*args)
    return (res[0], res[1]) if with_ctx else (res[0], None)


def _log_sigmoid(v):
    return -(jnp.maximum(-v, 0.0) + jnp.log1p(jnp.exp(-jnp.abs(v))))


RET_HEADS = 2
RET_UNROLL = 8


def _ret_kernel(*refs, with_ctx):
    if with_ctx:
        (ql_ref, kl_ref, vl_ref, gl_ref, qc_ref, kc_ref, vc_ref, gc_ref,
         cos_ref, sa_ref, sb_ref, decf_ref, decb_ref, gng_ref, gnb_ref,
         o_ref, oc_ref, acc_sc, of_sc, ob_sc, qf_sc, qb_sc, kv_sc, cst_sc) = refs
    else:
        (ql_ref, kl_ref, vl_ref, gl_ref, kc_ref, vc_ref,
         cos_ref, sa_ref, sb_ref, decf_ref, decb_ref, gng_ref, gnb_ref,
         o_ref, acc_sc, of_sc, ob_sc, qf_sc, qb_sc, kv_sc, cst_sc) = refs
        qc_ref = None

    n_ctx = CTX_LEN // CHUNK
    n_lat = SEQ // CHUNK
    n_all = n_ctx + n_lat
    ii = lax.broadcasted_iota(jnp.int32, (CHUNK, CHUNK), 0).astype(F32)
    jj = lax.broadcasted_iota(jnp.int32, (CHUNK, CHUNK), 1).astype(F32)
    dfw = ii - jj
    jrow = jj[0:1, :]
    cdec = []
    kdec_rows = []
    for hh in range(RET_HEADS):
        lg_f = _log_sigmoid(decf_ref[hh])
        lg_b = _log_sigmoid(decb_ref[hh])
        base = 4 * hh
        cst_sc[base + 0] = jnp.where(dfw >= 0, jnp.exp(jnp.maximum(dfw, 0.0) * lg_f), 0.0)
        cst_sc[base + 1] = jnp.where(dfw <= 0, jnp.exp(jnp.maximum(-dfw, 0.0) * lg_b), 0.0)
        cst_sc[base + 2] = jnp.exp((ii + 1.0) * lg_f)
        cst_sc[base + 3] = jnp.exp((CHUNK - ii) * lg_b)
        kdec_rows += [jnp.exp((CHUNK - 1.0 - jrow) * lg_f), jnp.exp(jrow * lg_b)]
        cdec += [jnp.exp(CHUNK * lg_f), jnp.exp(CHUNK * lg_b)]

    def intra(c, lat, need_out):
        rows = pl.ds(pl.multiple_of(c * CHUNK, CHUNK), CHUNK)
        src = pl.ds(pl.multiple_of((c - n_ctx) * CHUNK, CHUNK), CHUNK) if lat else rows
        if lat:
            cos, sa, sb = cos_ref[src, :], sa_ref[src, :], sb_ref[src, :]
        for hh in range(RET_HEADS):
            lanes = slice(hh * HEAD_DIM, (hh + 1) * HEAD_DIM)
            base = 4 * hh
            if lat:
                kc = _rope(kl_ref[src, lanes], cos, sa, sb) * ATTN_SCALE
                vc = vl_ref[src, lanes].astype(BF16)
            else:
                kc = kc_ref[src, lanes] * ATTN_SCALE
                vc = vc_ref[src, lanes].astype(BF16)
            if need_out:
                qc = _rope(ql_ref[src, lanes], cos, sa, sb) if lat else qc_ref[src, lanes]
                a = _dot_nt(qc.astype(BF16), kc.astype(BF16))
                att = jnp.concatenate([a * cst_sc[base], a * cst_sc[base + 1]], axis=0)
                o2 = jnp.dot(att.astype(BF16), vc, preferred_element_type=F32)
                acc_sc[hh, rows, :] = o2[:CHUNK] + o2[CHUNK:]
                qf_sc[hh, rows, :] = (qc * cst_sc[base + 2]).astype(BF16)
                qb_sc[hh, rows, :] = (qc * cst_sc[base + 3]).astype(BF16)
            kt = jnp.transpose(kc)
            kd2 = jnp.concatenate([kt * kdec_rows[2 * hh], kt * kdec_rows[2 * hh + 1]], axis=0)
            kv_sc[hh, c] = jnp.dot(kd2.astype(BF16), vc, preferred_element_type=F32)

    for c in range(n_ctx):
        intra(c, False, with_ctx)

    def intra_lat(c, carry):
        intra(c, True, True)
        return carry

    lax.fori_loop(n_ctx, n_all, intra_lat, 0, unroll=RET_UNROLL)

    def inter(hh, direction, c, s, q_ref_, out_ref, need_out):
        if need_out:
            rows = pl.ds(pl.multiple_of(c * CHUNK, CHUNK), CHUNK)
            out_ref[hh, rows, :] = jnp.dot(q_ref_[hh, rows, :], s.astype(BF16),
                                           preferred_element_type=F32)
        kv = kv_sc[hh, c, direction * CHUNK:(direction + 1) * CHUNK, :]
        return s * cdec[2 * hh + direction] + kv

    states = []
    for hh in range(RET_HEADS):
        sf = jnp.zeros((HEAD_DIM, HEAD_DIM), F32)
        sb = jnp.zeros((HEAD_DIM, HEAD_DIM), F32)
        for c in range(n_ctx):
            sf = inter(hh, 0, c, sf, qf_sc, of_sc, with_ctx)
            sb = inter(hh, 1, n_ctx - 1 - c, sb, qb_sc, ob_sc, with_ctx)
        states += [sf, sb]

    def lat_step(t, carry):
        new = []
        for hh in range(RET_HEADS):
            new.append(inter(hh, 0, n_ctx + t, carry[2 * hh], qf_sc, of_sc, True))
            new.append(inter(hh, 1, n_all - 1 - t, carry[2 * hh + 1], qb_sc, ob_sc, True))
        return tuple(new)

    lax.fori_loop(0, n_lat, lat_step, tuple(states), unroll=RET_UNROLL)

    def group_norm(o, gate, lanes):
        mu = jnp.mean(o, axis=-1, keepdims=True)
        var = jnp.mean(jnp.square(o - mu), axis=-1, keepdims=True)
        on = (o - mu) * lax.rsqrt(var + LN_EPS)
        return (on * gng_ref[:, lanes] + gnb_ref[:, lanes]) * _silu(gate)

    for hh in range(RET_HEADS):
        lanes = slice(hh * HEAD_DIM, (hh + 1) * HEAD_DIM)
        o = acc_sc[hh, CTX_LEN:, :] + of_sc[hh, CTX_LEN:, :] + ob_sc[hh, CTX_LEN:, :]
        o_ref[:, lanes] = group_norm(o, gl_ref[:, lanes], lanes).astype(o_ref.dtype)
        if with_ctx:
            oc = acc_sc[hh, 0:CTX_LEN, :] + of_sc[hh, 0:CTX_LEN, :] + ob_sc[hh, 0:CTX_LEN, :]
            oc_ref[:, lanes] = group_norm(oc, gc_ref[:, lanes], lanes).astype(oc_ref.dtype)


def _retention_mixer(p_lat, p_ctx, ctx_cols, rope_tabs, dec_f, dec_b, gn_g3, gn_b3, l, with_ctx):
    cos, sin_a, sin_b = rope_tabs
    kern = functools.partial(_ret_kernel, with_ctx=with_ctx)
    width = RET_HEADS * HEAD_DIM

    def col(c0, rows):
        return pl.BlockSpec((rows, width), lambda b, h: (b, c0 // RET_HEADS + h))

    in_specs = [col(COL_C_Q, SEQ), col(COL_C_K, SEQ), col(COL_C_V, SEQ), col(COL_C_GATE, SEQ)]
    args = [p_lat, p_lat, p_lat, p_lat]
    if with_ctx:
        in_specs += [col(COL_C_Q, CTX_LEN), col(COL_C_K, CTX_LEN), col(COL_C_V, CTX_LEN),
                     col(COL_C_GATE, CTX_LEN)]
        args += [p_ctx, p_ctx, p_ctx, p_ctx]
    else:
        in_specs += [col(ctx_cols["c_k"], CTX_LEN), col(ctx_cols["c_v"], CTX_LEN)]
        args += [p_ctx, p_ctx]
    tab = pl.BlockSpec((SEQ, HEAD_DIM), lambda b, h: (0, 0))
    dec = pl.BlockSpec((None, RET_HEADS, 1, HEAD_DIM), lambda b, h: (l, h, 0, 0))
    gvec = pl.BlockSpec((None, 1, width), lambda b, h: (l, 0, h))
    in_specs += [tab, tab, tab, dec, dec, gvec, gvec]
    args += [cos, sin_a, sin_b, dec_f, dec_b, gn_g3, gn_b3]
    out_specs = [pl.BlockSpec((SEQ, width), lambda b, h: (b, h))]
    out_shape = [jax.ShapeDtypeStruct((BATCH * SEQ, GROUP_W), BF16)]
    if with_ctx:
        out_specs.append(pl.BlockSpec((CTX_LEN, width), lambda b, h: (b, h)))
        out_shape.append(jax.ShapeDtypeStruct((BATCH * CTX_LEN, GROUP_W), BF16))
    n_rows = CTX_LEN + SEQ
    rows3 = (RET_HEADS, n_rows, HEAD_DIM)
    res = pl.pallas_call(
        kern,
        grid=(BATCH, N_HEADS // RET_HEADS),
        in_specs=in_specs,
        out_specs=out_specs,
        out_shape=out_shape,
        scratch_shapes=[pltpu.VMEM(rows3, F32), pltpu.VMEM(rows3, F32), pltpu.VMEM(rows3, F32),
                        pltpu.VMEM(rows3, BF16), pltpu.VMEM(rows3, BF16),
                        pltpu.VMEM((RET_HEADS, n_rows // CHUNK, 2 * CHUNK, HEAD_DIM), F32),
                        pltpu.VMEM((RET_HEADS * 4, CHUNK, CHUNK), F32)],
        compiler_params=_params(("arbitrary", "arbitrary")),
        name="retention_mixer",
    )(*args)
    return (res[0], res[1]) if with_ctx else (res[0], None)


def _na_row_start(r):
    return min(max(r - NA_WIN_R // 2, 0), GRID_H - NA_WIN_R)


def _na_key_start(blk):
    return min(max(blk * NA_BLK_ROWS - NA_WIN_R // 2, 0), GRID_H - NA_BLK_KROWS)


def _na_table_kernel(tab_ref, bias_ref, mask_ref):
    q = lax.broadcasted_iota(jnp.int32, (GRID_W, HEAD_DIM), 0)
    lane = lax.broadcasted_iota(jnp.int32, (GRID_W, HEAD_DIM), 1)
    kcol = lane % GRID_W
    lower_half = lane < GRID_W
    off_c = jnp.clip(kcol - q, -(NA_WIN_C - 1), NA_WIN_C - 1) + (NA_WIN_C - 1)
    c_start = jnp.clip(q - NA_WIN_C // 2, 0, GRID_W - NA_WIN_C)
    col_win = jnp.where((kcol >= c_start) & (kcol < c_start + NA_WIN_C), 1.0, 0.0)
    zeros = jnp.zeros((GRID_W, HEAD_DIM), F32)
    planes = []
    for o in range(2 * NA_WIN_R - 1):
        g = zeros
        for c in range(2 * NA_WIN_C - 1):
            g = jnp.where(off_c == c, tab_ref[0, o, c] * LOG2E, g)
        planes.append(g)
    n_blk = GRID_H // NA_BLK_ROWS
    for pat, blk in enumerate((0, 1, n_blk - 1)):
        k0 = _na_key_start(blk)
        for rq in range(NA_BLK_ROWS):
            r = blk * NA_BLK_ROWS + rq
            rs = _na_row_start(r)
            for jp in range(NA_BLK_KROWS // 2):
                b_half, m_half = [], []
                for key_row in (k0 + 2 * jp, k0 + 2 * jp + 1):
                    valid = rs <= key_row < rs + NA_WIN_R
                    b_half.append(planes[key_row - r + NA_WIN_R - 1] if valid else zeros)
                    m_half.append(col_win if valid else zeros)
                rows = slice(rq * GRID_W, (rq + 1) * GRID_W)
                lanes = slice(jp * HEAD_DIM, (jp + 1) * HEAD_DIM)
                bias_ref[0, pat, rows, lanes] = jnp.where(lower_half, b_half[0], b_half[1])
                mask_ref[0, pat, rows, lanes] = jnp.where(lower_half, m_half[0], m_half[1])


def _na_tables(na_bias_l):
    spec = pl.BlockSpec((1, NA_PATTERNS, NA_BLK_Q, NA_BLK_KEYS), lambda h: (h, 0, 0, 0))
    shape = jax.ShapeDtypeStruct((N_HEADS, NA_PATTERNS, NA_BLK_Q, NA_BLK_KEYS), F32)
    return pl.pallas_call(
        _na_table_kernel,
        grid=(N_HEADS,),
        in_specs=[pl.BlockSpec((1, 2 * NA_WIN_R - 1, 2 * NA_WIN_C - 1), lambda h: (h, 0, 0),
                               memory_space=pltpu.SMEM)],
        out_specs=[spec, spec],
        out_shape=[shape, shape],
        compiler_params=_params(("arbitrary",)),
        name="na_tables",
    )(na_bias_l)


def _na_kernel(*refs, with_ctx):
    if with_ctx:
        (q_ref, k_ref, v_ref, g_ref, qc_ref, kc_ref, vc_ref, gc_ref, bias_ref, mask_ref,
         o_ref, oc_ref, q_sc, k_sc, v_sc, kc_sc, vc_sc, scx_sc) = refs
    else:
        (q_ref, k_ref, v_ref, g_ref, kc_ref, vc_ref, bias_ref, mask_ref,
         o_ref, q_sc, k_sc, v_sc, kc_sc, vc_sc, scx_sc) = refs
    scale = ATTN_SCALE * LOG2E
    for hh in range(NA_HEADS):
        lanes = slice(hh * HEAD_DIM, (hh + 1) * HEAD_DIM)
        q_sc[hh] = (q_ref[:, lanes] * scale).astype(BF16)
        k_sc[hh] = k_ref[:, lanes].astype(BF16)
        kc_sc[hh] = kc_ref[:, lanes].astype(BF16)
        v_sc[hh, :, 0:HEAD_DIM] = v_ref[:, lanes].astype(BF16)
        v_sc[hh, :, HEAD_DIM:] = jnp.ones((SEQ, HEAD_DIM), BF16)
        vc_sc[hh, :, 0:HEAD_DIM] = vc_ref[:, lanes].astype(BF16)
        vc_sc[hh, :, HEAD_DIM:] = jnp.ones((CTX_LEN, HEAD_DIM), BF16)
        scx_sc[hh] = _dot_nt(q_sc[hh], kc_sc[hh])
    n_blk = GRID_H // NA_BLK_ROWS

    def blk_step(blk, carry):
        k0 = jnp.clip(blk * NA_BLK_ROWS - NA_WIN_R // 2, 0, GRID_H - NA_BLK_KROWS)
        pat = jnp.where(blk == 0, 0, jnp.where(blk == n_blk - 1, 2, 1))
        rows = pl.ds(pl.multiple_of(blk * NA_BLK_Q, NA_BLK_Q), NA_BLK_Q)
        keys = pl.ds(pl.multiple_of(k0 * GRID_W, GRID_W), NA_BLK_KEYS)
        in_win = mask_ref[0, pat] > 0.5
        for hh in range(NA_HEADS):
            lanes = slice(hh * HEAD_DIM, (hh + 1) * HEAD_DIM)
            s_nb = _dot_nt(q_sc[hh, rows, :], k_sc[hh, keys, :]) + bias_ref[hh, pat]
            s_nb = jnp.where(in_win, s_nb, MASK_VALUE)
            s_cx = scx_sc[hh, rows, :]
            m = jnp.maximum(s_nb.max(axis=-1, keepdims=True), s_cx.max(axis=-1, keepdims=True))
            o2 = jnp.dot(jnp.exp2(s_nb - m).astype(BF16), v_sc[hh, keys, :],
                         preferred_element_type=F32)
            o2 = o2 + jnp.dot(jnp.exp2(s_cx - m).astype(BF16), vc_sc[hh],
                              preferred_element_type=F32)
            o = o2[:, 0:HEAD_DIM] / o2[:, HEAD_DIM:HEAD_DIM + 1]
            o_ref[rows, lanes] = (o * _silu(g_ref[rows, lanes])).astype(o_ref.dtype)
        return carry

    lax.fori_loop(0, n_blk, blk_step, 0, unroll=NA_UNROLL)

    if with_ctx:
        for hh in range(NA_HEADS):
            lanes = slice(hh * HEAD_DIM, (hh + 1) * HEAD_DIM)
            s = _dot_nt((qc_ref[:, lanes] * scale).astype(BF16), kc_sc[hh])
            o = _softmax_pv([s], [vc_sc[hh, :, 0:HEAD_DIM]])
            oc_ref[:, lanes] = (o * _silu(gc_ref[:, lanes])).astype(oc_ref.dtype)


def _na_mixer(p_lat, p_ctx, ctx_cols, na_tabs, with_ctx):
    kern = functools.partial(_na_kernel, with_ctx=with_ctx)

    width = NA_HEADS * HEAD_DIM

    def col(c0, rows):
        return pl.BlockSpec((rows, width), lambda b, h: (b, c0 // NA_HEADS + h))

    in_specs = [col(COL_D_Q, SEQ), col(COL_D_K, SEQ), col(COL_D_V, SEQ), col(COL_D_GATE, SEQ)]
    args = [p_lat, p_lat, p_lat, p_lat]
    if with_ctx:
        in_specs += [col(COL_D_Q, CTX_LEN), col(COL_D_K, CTX_LEN), col(COL_D_V, CTX_LEN),
                     col(COL_D_GATE, CTX_LEN)]
        args += [p_ctx, p_ctx, p_ctx, p_ctx]
    else:
        in_specs += [col(ctx_cols["d_k"], CTX_LEN), col(ctx_cols["d_v"], CTX_LEN)]
        args += [p_ctx, p_ctx]
    tab_shape = (NA_PATTERNS, NA_BLK_Q, NA_BLK_KEYS)
    in_specs += [pl.BlockSpec((NA_HEADS,) + tab_shape, lambda b, h: (h, 0, 0, 0)),
                 pl.BlockSpec((1,) + tab_shape, lambda b, h: (0, 0, 0, 0))]
    args += list(na_tabs)
    out_specs = [pl.BlockSpec((SEQ, width), lambda b, h: (b, h))]
    out_shape = [jax.ShapeDtypeStruct((BATCH * SEQ, GROUP_W), BF16)]
    if with_ctx:
        out_specs.append(pl.BlockSpec((CTX_LEN, width), lambda b, h: (b, h)))
        out_shape.append(jax.ShapeDtypeStruct((BATCH * CTX_LEN, GROUP_W), BF16))
    res = pl.pallas_call(
        kern,
        grid=(BATCH, N_HEADS // NA_HEADS),
        in_specs=in_specs,
        out_specs=out_specs,
        out_shape=out_shape,
        scratch_shapes=[pltpu.VMEM((NA_HEADS, SEQ, HEAD_DIM), BF16),
                        pltpu.VMEM((NA_HEADS, SEQ, HEAD_DIM), BF16),
                        pltpu.VMEM((NA_HEADS, SEQ, 2 * HEAD_DIM), BF16),
                        pltpu.VMEM((NA_HEADS, CTX_LEN, HEAD_DIM), BF16),
                        pltpu.VMEM((NA_HEADS, CTX_LEN, 2 * HEAD_DIM), BF16),
                        pltpu.VMEM((NA_HEADS, SEQ, CTX_LEN), F32)],
        compiler_params=_params(("arbitrary", "arbitrary")),
        name="na_mixer",
    )(*args)
    return (res[0], res[1]) if with_ctx else (res[0], None)


def _rope_tables():
    t = jnp.arange(SEQ)
    row = (t // GRID_W).astype(F32)
    col = (t % GRID_W).astype(F32)
    half = HEAD_DIM // 2
    inv_freq = ROPE_THETA ** (-jnp.arange(0, half, 2, dtype=F32) / half)
    ang_r = row[:, None] * inv_freq[None, :]
    ang_c = col[:, None] * inv_freq[None, :]
    ang = jnp.concatenate([ang_r, ang_r, ang_c, ang_c], axis=-1)
    cos, sin = jnp.cos(ang), jnp.sin(ang)
    quarter = (jnp.arange(HEAD_DIM) // (HEAD_DIM // 4)) % 2
    sin_a = jnp.where(quarter[None, :] == 0, -sin, 0.0)
    sin_b = jnp.where(quarter[None, :] == 1, sin, 0.0)
    return cos, sin_a, sin_b


def kernel(x, c, ctx, c_ctx, ada_w, ada_b, norm_g, w_in, conv_w, conv_b, conv_ln_g, conv_ln_b,
           gqa_qn_g, gqa_kn_g, ret_decay_fwd, ret_decay_bwd, ret_gn_g, ret_gn_b, na_bias,
           w_out, final_g):
    rope_tabs = _rope_tables()
    x_lat = x.reshape(BATCH * SEQ, D_MODEL)
    x_ctx = ctx.reshape(BATCH * CTX_LEN, D_MODEL)
    c8 = jnp.concatenate(
        [c, c_ctx[None, :], jnp.zeros((MOD_ROWS - BATCH - 1, D_MODEL), F32)], axis=0)

    def vec3(a):
        return a.reshape(DEPTH, 1, a.shape[-1])

    ada_b3, norm_g3 = vec3(ada_b), vec3(norm_g)
    conv_b3, ln_g3, ln_b3 = vec3(conv_b), vec3(conv_ln_g), vec3(conv_ln_b)
    qn3, kn3 = vec3(gqa_qn_g), vec3(gqa_kn_g)
    gn_g3, gn_b3 = vec3(ret_gn_g), vec3(ret_gn_b)
    dec_f = jnp.broadcast_to(ret_decay_fwd[:, :, None, None], (DEPTH, N_HEADS, 1, HEAD_DIM))
    dec_b = jnp.broadcast_to(ret_decay_bwd[:, :, None, None], (DEPTH, N_HEADS, 1, HEAD_DIM))

    for l in range(DEPTH):
        with_ctx = l < DEPTH - 1
        mod = _ada_table(c8, ada_w, ada_b3, l)
        h_lat = _norm_modulate(x_lat, norm_g3, l, mod, SEQ, None)
        h_ctx = _norm_modulate(x_ctx, norm_g3, l, mod, None, MOD_CTX_ROW)
        p_lat = _in_proj(h_lat, w_in, l)
        p_ctx = _in_proj(h_ctx, w_in, l, kv_only=not with_ctx)
        ctx_cols = CTX_COLS_FULL if with_ctx else CTX_COLS_KV_ONLY

        ya_l = _conv_mixer(p_lat, SEQ, conv_w, conv_b3, ln_g3, ln_b3, l)
        yb_l, yb_c = _gqa_mixer(p_lat, p_ctx, ctx_cols, rope_tabs, qn3, kn3, l, with_ctx)
        yc_l, yc_c = _retention_mixer(p_lat, p_ctx, ctx_cols, rope_tabs, dec_f, dec_b, gn_g3, gn_b3,
                                      l, with_ctx)
        yd_l, yd_c = _na_mixer(p_lat, p_ctx, ctx_cols, _na_tables(na_bias[l]), with_ctx)

        x_new = _out_proj((ya_l, yb_l, yc_l, yd_l), w_out, l, x_lat, mod, SEQ, None)
        if with_ctx:
            ya_c = _conv_mixer(p_ctx, CTX_LEN, conv_w, conv_b3, ln_g3, ln_b3, l)
            x_ctx = _out_proj((ya_c, yb_c, yc_c, yd_c), w_out, l, x_ctx, mod, None, MOD_CTX_ROW)
        x_lat = x_new

    out = _final_norm(x_lat, final_g.reshape(1, D_MODEL))
    return out.reshape(BATCH, SEQ, D_MODEL)
```

```python
import functools

import jax
import jax.numpy as jnp
from jax import lax
from jax.experimental import pallas as pl
from jax.experimental.pallas import tpu as pltpu

F32 = jnp.float32
BF16 = jnp.bfloat16

D_MODEL = 4096
BATCH = 4
SEQ = 2048
DEPTH = 2
GRID_W = 64
GRID_H = SEQ // GRID_W
CTX_LEN = 256
GROUP_W = 1024
HEAD_DIM = 128
N_HEADS = GROUP_W // HEAD_DIM
KV_HEADS = 2
Q_PER_KV = N_HEADS // KV_HEADS
CONV_WIDTH = 31
CONV_HALF = CONV_WIDTH // 2
CHUNK = 128
NA_WIN_R = 8
NA_WIN_C = 16
ROPE_THETA = 10000.0
NORM_EPS = 1e-6
LN_EPS = 1e-5
ATTN_SCALE = HEAD_DIM ** -0.5
MASK_VALUE = -1e30
P_TOTAL = 13 * GROUP_W + 2 * KV_HEADS * HEAD_DIM

COL_A_GLU = 0
COL_A_GATE = 16
COL_B_Q = 24
COL_B_K = 32
COL_B_V = 34
COL_B_GATE = 36
COL_C_Q = 44
COL_C_K = 52
COL_C_V = 60
COL_C_GATE = 68
COL_D_Q = 76
COL_D_K = 84
COL_D_V = 92
COL_D_GATE = 100

NORM_TM = 512
MOD_ROWS = 8
MOD_CTX_ROW = BATCH
VMEM_LIMIT = 56 * 1024 * 1024

NA_BLK_ROWS = 4
NA_BLK_KROWS = 12
NA_BLK_Q = NA_BLK_ROWS * GRID_W
NA_BLK_KEYS = NA_BLK_KROWS * GRID_W
NA_PATTERNS = 3
NA_UNROLL = 4
NA_HEADS = 2


def _params(sem, flags=None):
    return pltpu.CompilerParams(dimension_semantics=sem, vmem_limit_bytes=VMEM_LIMIT, flags=flags)


def _silu(v):
    return v * jax.nn.sigmoid(v)


def _mod_kernel(c_ref, w_ref, b_ref, o_ref):
    a = _silu(c_ref[...]).astype(BF16)
    o_ref[...] = jnp.dot(a, w_ref[...].astype(BF16), preferred_element_type=F32) + b_ref[...]


def _ada_table(c8, ada_w, ada_b3, l):
    tn = 512
    n = 3 * D_MODEL
    return pl.pallas_call(
        _mod_kernel,
        grid=(n // tn,),
        in_specs=[
            pl.BlockSpec((MOD_ROWS, D_MODEL), lambda j: (0, 0)),
            pl.BlockSpec((None, D_MODEL, tn), lambda j: (l, 0, j)),
            pl.BlockSpec((None, 1, tn), lambda j: (l, 0, j)),
        ],
        out_specs=pl.BlockSpec((MOD_ROWS, tn), lambda j: (0, j)),
        out_shape=jax.ShapeDtypeStruct((MOD_ROWS, n), F32),
        compiler_params=_params(("arbitrary",)),
        name="ada_table",
    )(c8, ada_w, ada_b3)


def _normmod_kernel(x_ref, g_ref, shift_ref, scale_ref, o_ref, *, tiles_per_row, fixed_row):
    if fixed_row is None:
        r = pl.program_id(0) // tiles_per_row
    else:
        r = fixed_row
    x = x_ref[...]
    y = x * lax.rsqrt(jnp.mean(x * x, axis=-1, keepdims=True) + NORM_EPS)
    y = y * g_ref[...]
    shift = shift_ref[pl.ds(r, 1), :]
    scale = scale_ref[pl.ds(r, 1), :]
    o_ref[...] = (y * (1.0 + scale) + shift).astype(o_ref.dtype)


def _norm_modulate(x2, norm_g3, l, mod, rows_per_mod, fixed_row):
    m = x2.shape[0]
    tm = NORM_TM
    tiles_per_row = None if rows_per_mod is None else rows_per_mod // tm
    kern = functools.partial(_normmod_kernel, tiles_per_row=tiles_per_row, fixed_row=fixed_row)
    return pl.pallas_call(
        kern,
        grid=(m // tm,),
        in_specs=[
            pl.BlockSpec((tm, D_MODEL), lambda i: (i, 0)),
            pl.BlockSpec((None, 1, D_MODEL), lambda i: (l, 0, 0)),
            pl.BlockSpec((MOD_ROWS, D_MODEL), lambda i: (0, 0)),
            pl.BlockSpec((MOD_ROWS, D_MODEL), lambda i: (0, 1)),
        ],
        out_specs=pl.BlockSpec((tm, D_MODEL), lambda i: (i, 0)),
        out_shape=jax.ShapeDtypeStruct((m, D_MODEL), BF16),
        compiler_params=_params(("arbitrary",)),
        name="norm_modulate",
    )(x2, norm_g3, mod, mod)


PROJ_TM = 1024
PROJ_TN = 512
IN_PROJ_TN = PROJ_TN


def _stage_weight_chunk(w_ref, wbf_ref, n_cols):
    j, i = pl.program_id(0), pl.program_id(1)
    chunk = w_ref.shape[0]

    @pl.when(j < n_cols)
    def _():
        wbf_ref[j % 2, pl.ds(pl.multiple_of(i * chunk, chunk), chunk), :] = w_ref[...].astype(BF16)

    return j, (j + 1) % 2


def _mm_kernel(a_ref, w_ref, o_ref, wbf_ref, *, n_cols):
    j, slot = _stage_weight_chunk(w_ref, wbf_ref, n_cols)

    @pl.when(j > 0)
    def _():
        o_ref[...] = jnp.dot(a_ref[...], wbf_ref[slot], preferred_element_type=F32)


def _ctx_kv_col_tile(j):
    b_kv = COL_B_K * HEAD_DIM // PROJ_TN
    c_kv = COL_C_K * HEAD_DIM // PROJ_TN
    d_kv = COL_D_K * HEAD_DIM // PROJ_TN
    n_c = 2 * GROUP_W // PROJ_TN
    return jnp.where(j < 1, b_kv, jnp.where(j < 1 + n_c, c_kv + (j - 1), d_kv + (j - 1 - n_c)))


CTX_KV_TILES = 1 + 2 * (2 * GROUP_W // PROJ_TN)
CTX_COLS_FULL = dict(b_k=COL_B_K, b_v=COL_B_V, c_k=COL_C_K, c_v=COL_C_V, d_k=COL_D_K, d_v=COL_D_V)
CTX_COLS_KV_ONLY = dict(b_k=0, b_v=KV_HEADS, c_k=2 * KV_HEADS, c_v=2 * KV_HEADS + N_HEADS,
                        d_k=2 * KV_HEADS + 2 * N_HEADS, d_v=2 * KV_HEADS + 3 * N_HEADS)


def _mm_single_kernel(a_ref, w_ref, o_ref):
    o_ref[...] = jnp.dot(a_ref[...], w_ref[...].astype(BF16), preferred_element_type=F32)


def _in_proj_single(h, w_in, l, kv_only):
    tm, tn = PROJ_TM, PROJ_TN
    n_tiles = CTX_KV_TILES if kv_only else P_TOTAL // tn
    col = _ctx_kv_col_tile if kv_only else (lambda j: j)
    return pl.pallas_call(
        _mm_single_kernel,
        grid=(n_tiles,),
        in_specs=[
            pl.BlockSpec((tm, D_MODEL), lambda j: (0, 0)),
            pl.BlockSpec((None, D_MODEL, tn), lambda j: (l, 0, col(j))),
        ],
        out_specs=pl.BlockSpec((tm, tn), lambda j: (0, j)),
        out_shape=jax.ShapeDtypeStruct((tm, n_tiles * tn), F32),
        compiler_params=_params(("arbitrary",)),
        name="in_proj_ctx",
    )(h, w_in)


def _in_proj(h, w_in, l, kv_only=False):
    m = h.shape[0]
    if m == PROJ_TM:
        return _in_proj_single(h, w_in, l, kv_only)
    tm, tn = PROJ_TM, IN_PROJ_TN
    n_tiles = P_TOTAL // tn
    n_rows = m // tm
    chunk = D_MODEL // n_rows
    return pl.pallas_call(
        functools.partial(_mm_kernel, n_cols=n_tiles),
        grid=(n_tiles + 1, n_rows),
        in_specs=[
            pl.BlockSpec((tm, D_MODEL), lambda j, i: (jnp.where(j == 0, 0, i), 0)),
            pl.BlockSpec((None, chunk, tn), lambda j, i: (l, i, jnp.minimum(j, n_tiles - 1))),
        ],
        out_specs=pl.BlockSpec((tm, tn), lambda j, i: (jnp.where(j == 0, 0, i), jnp.maximum(j - 1, 0))),
        out_shape=jax.ShapeDtypeStruct((m, n_tiles * tn), F32),
        scratch_shapes=[pltpu.VMEM((2, D_MODEL, tn), BF16)],
        compiler_params=_params(("arbitrary", "arbitrary")),
        name="in_proj",
    )(h, w_in)


def _out_kernel(ya_ref, yb_ref, yc_ref, yd_ref, w_ref, x_ref, gate_ref, o_ref, wbf_ref, *,
                tiles_per_row, fixed_row, n_cols):
    if fixed_row is None:
        r = pl.program_id(1) // tiles_per_row
    else:
        r = fixed_row
    j, slot = _stage_weight_chunk(w_ref, wbf_ref, n_cols)

    @pl.when(j > 0)
    def _():
        acc = None
        for g, y_ref in enumerate((ya_ref, yb_ref, yc_ref, yd_ref)):
            part = jnp.dot(y_ref[...], wbf_ref[slot, g * GROUP_W:(g + 1) * GROUP_W, :],
                           preferred_element_type=F32)
            acc = part if acc is None else acc + part
        gate = gate_ref[pl.ds(r, 1), :]
        o_ref[...] = x_ref[...] + gate * acc


def _out_single_kernel(ya_ref, yb_ref, yc_ref, yd_ref, w_ref, x_ref, gate_ref, o_ref, *, fixed_row):
    acc = None
    for g, y_ref in enumerate((ya_ref, yb_ref, yc_ref, yd_ref)):
        part = jnp.dot(y_ref[...], w_ref[g * GROUP_W:(g + 1) * GROUP_W, :].astype(BF16),
                       preferred_element_type=F32)
        acc = part if acc is None else acc + part
    o_ref[...] = x_ref[...] + gate_ref[fixed_row:fixed_row + 1, :] * acc


def _out_proj_single(ys, w_out, l, x2, mod, fixed_row):
    tm, tn = PROJ_TM, PROJ_TN
    y_spec = pl.BlockSpec((tm, GROUP_W), lambda j: (0, 0))
    gate_col0 = 2 * D_MODEL // tn
    return pl.pallas_call(
        functools.partial(_out_single_kernel, fixed_row=fixed_row),
        grid=(D_MODEL // tn,),
        in_specs=[
            y_spec, y_spec, y_spec, y_spec,
            pl.BlockSpec((None, D_MODEL, tn), lambda j: (l, 0, j)),
            pl.BlockSpec((tm, tn), lambda j: (0, j)),
            pl.BlockSpec((MOD_ROWS, tn), lambda j: (0, gate_col0 + j)),
        ],
        out_specs=pl.BlockSpec((tm, tn), lambda j: (0, j)),
        out_shape=jax.ShapeDtypeStruct((tm, D_MODEL), F32),
        compiler_params=_params(("arbitrary",)),
        name="out_proj_ctx",
    )(*ys, w_out, x2, mod)


def _out_proj(ys, w_out, l, x2, mod, rows_per_mod, fixed_row):
    m = x2.shape[0]
    if m == PROJ_TM:
        return _out_proj_single(ys, w_out, l, x2, mod, fixed_row)
    tm, tn = PROJ_TM, PROJ_TN
    tiles_per_row = None if rows_per_mod is None else rows_per_mod // tm
    n_cols = D_MODEL // tn
    n_rows = m // tm
    chunk = D_MODEL // n_rows
    kern = functools.partial(_out_kernel, tiles_per_row=tiles_per_row, fixed_row=fixed_row,
                             n_cols=n_cols)

    def out_col(j):
        return jnp.maximum(j - 1, 0)

    def row(j, i):
        return jnp.where(j == 0, 0, i)

    y_spec = pl.BlockSpec((tm, GROUP_W), lambda j, i: (row(j, i), 0))
    gate_col0 = 2 * D_MODEL // tn
    return pl.pallas_call(
        kern,
        grid=(n_cols + 1, n_rows),
        in_specs=[
            y_spec, y_spec, y_spec, y_spec,
            pl.BlockSpec((None, chunk, tn), lambda j, i: (l, i, jnp.minimum(j, n_cols - 1))),
            pl.BlockSpec((tm, tn), lambda j, i: (row(j, i), out_col(j))),
            pl.BlockSpec((MOD_ROWS, tn), lambda j, i: (0, gate_col0 + out_col(j))),
        ],
        out_specs=pl.BlockSpec((tm, tn), lambda j, i: (row(j, i), out_col(j))),
        out_shape=jax.ShapeDtypeStruct((m, D_MODEL), F32),
        scratch_shapes=[pltpu.VMEM((2, D_MODEL, tn), BF16)],
        compiler_params=_params(("arbitrary", "arbitrary")),
        name="out_proj",
    )(*ys, w_out, x2, mod)


def _final_norm_kernel(x_ref, g_ref, o_ref):
    x = x_ref[...]
    o_ref[...] = x * lax.rsqrt(jnp.mean(x * x, axis=-1, keepdims=True) + NORM_EPS) * g_ref[...]


def _final_norm(x2, g2):
    m = x2.shape[0]
    tm = NORM_TM
    return pl.pallas_call(
        _final_norm_kernel,
        grid=(m // tm,),
        in_specs=[pl.BlockSpec((tm, D_MODEL), lambda i: (i, 0)),
                  pl.BlockSpec((1, D_MODEL), lambda i: (0, 0))],
        out_specs=pl.BlockSpec((tm, D_MODEL), lambda i: (i, 0)),
        out_shape=jax.ShapeDtypeStruct((m, D_MODEL), F32),
        compiler_params=_params(("arbitrary",)),
        name="final_norm",
    )(x2, g2)


CONV_HALO = 16
CONV_SUB = 64
SUBLANES = 8


def _conv_kernel(cur_ref, prev_ref, next_ref, gate_ref, w_ref, b_ref, lng_ref, lnb_ref, o_ref,
                 u_ref, ush_ref, c_ref, *, tile, n_tiles):
    t = pl.program_id(1)

    def glu(ref):
        return ref[:, :GROUP_W] * jax.nn.sigmoid(ref[:, GROUP_W:])

    u_ref[CONV_HALO:CONV_HALO + tile, :] = glu(cur_ref)
    u_ref[0:CONV_HALO, :] = jnp.where(t > 0, glu(prev_ref), 0.0)
    u_ref[CONV_HALO + tile:, :] = jnp.where(t < n_tiles - 1, glu(next_ref), 0.0)

    sh_rows = ush_ref.shape[1]
    for b in range(1, SUBLANES):
        ush_ref[b - 1] = u_ref[b:b + sh_rows, :]

    base = CONV_HALO - CONV_HALF

    for s in range(tile // CONV_SUB):
        row0 = s * CONV_SUB
        for c in range(GROUP_W // HEAD_DIM):
            lanes = slice(c * HEAD_DIM, (c + 1) * HEAD_DIM)
            acc = jnp.zeros((CONV_SUB, HEAD_DIM), F32)
            for k in range(CONV_WIDTH):
                b = (base + k) % SUBLANES
                rows = slice(row0 + base + k - b, row0 + base + k - b + CONV_SUB)
                tap = u_ref[rows, lanes] if b == 0 else ush_ref[b - 1, rows, lanes]
                acc = acc + w_ref[k:k + 1, lanes] * tap
            c_ref[row0:row0 + CONV_SUB, lanes] = acc + b_ref[:, lanes]

    v = c_ref[...]
    mu = jnp.mean(v, axis=-1, keepdims=True)
    var = jnp.mean(jnp.square(v - mu), axis=-1, keepdims=True)
    y = (v - mu) * lax.rsqrt(var + LN_EPS) * lng_ref[...] + lnb_ref[...]
    o_ref[...] = (_silu(y) * _silu(gate_ref[...])).astype(o_ref.dtype)


def _conv_mixer(p, seq_len, conv_w, conv_b3, ln_g3, ln_b3, l):
    tile = 256
    n_tiles = seq_len // tile
    n_seq = p.shape[0] // seq_len
    halo_per_tile = tile // CONV_HALO
    last_halo = p.shape[0] // CONV_HALO - 1
    kern = functools.partial(_conv_kernel, tile=tile, n_tiles=n_tiles)
    vec = pl.BlockSpec((None, 1, GROUP_W), lambda b, t: (l, 0, 0))
    return pl.pallas_call(
        kern,
        grid=(n_seq, n_tiles),
        in_specs=[
            pl.BlockSpec((tile, 2 * GROUP_W), lambda b, t: (b * n_tiles + t, 0)),
            pl.BlockSpec((CONV_HALO, 2 * GROUP_W),
                         lambda b, t: (jnp.maximum((b * n_tiles + t) * halo_per_tile - 1, 0), 0)),
            pl.BlockSpec((CONV_HALO, 2 * GROUP_W),
                         lambda b, t: (jnp.minimum((b * n_tiles + t + 1) * halo_per_tile, last_halo), 0)),
            pl.BlockSpec((tile, GROUP_W), lambda b, t: (b * n_tiles + t, COL_A_GATE // 8)),
            pl.BlockSpec((None, CONV_WIDTH, GROUP_W), lambda b, t: (l, 0, 0)),
            vec, vec, vec,
        ],
        out_specs=pl.BlockSpec((tile, GROUP_W), lambda b, t: (b * n_tiles + t, 0)),
        out_shape=jax.ShapeDtypeStruct((p.shape[0], GROUP_W), BF16),
        scratch_shapes=[pltpu.VMEM((tile + 2 * CONV_HALO, GROUP_W), F32),
                        pltpu.VMEM((SUBLANES - 1, tile + 2 * CONV_HALO - SUBLANES, GROUP_W), F32),
                        pltpu.VMEM((tile, GROUP_W), F32)],
        compiler_params=_params(("arbitrary", "arbitrary")),
        name="conv_mixer",
    )(p, p, p, p, conv_w, conv_b3, ln_g3, ln_b3)


def _head_rmsnorm(v, g):
    return v * lax.rsqrt(jnp.mean(v * v, axis=-1, keepdims=True) + NORM_EPS) * g


def _rope(v, cos, sin_a, sin_b):
    return (v * cos + pltpu.roll(v, HEAD_DIM - HEAD_DIM // 4, 1) * sin_a
            + pltpu.roll(v, HEAD_DIM // 4, 1) * sin_b)


def _dot_nt(a, b):
    return lax.dot_general(a, b, (((1,), (1,)), ((), ())), preferred_element_type=F32)


LOG2E = 1.4426950408889634


def _softmax_pv(s_list, v_list):
    m = s_list[0].max(axis=-1, keepdims=True)
    for s in s_list[1:]:
        m = jnp.maximum(m, s.max(axis=-1, keepdims=True))
    den = None
    acc = None
    for s, v in zip(s_list, v_list):
        e = jnp.exp2(s - m)
        d = e.sum(axis=-1, keepdims=True)
        o = jnp.dot(e.astype(BF16), v, preferred_element_type=F32)
        den = d if den is None else den + d
        acc = o if acc is None else acc + o
    return acc / den


def _gqa_kernel(*refs, tq, with_ctx):
    if with_ctx:
        (q_ref, kl_ref, vl_ref, kc_ref, vc_ref, g_ref, qc_ref, gc_ref,
         cos_q, sa_q, sb_q, cos_k, sa_k, sb_k, qn_ref, kn_ref,
         o_ref, oc_ref, k_sc, v_sc) = refs
    else:
        (q_ref, kl_ref, vl_ref, kc_ref, vc_ref, g_ref,
         cos_q, sa_q, sb_q, cos_k, sa_k, sb_k, qn_ref, kn_ref,
         o_ref, k_sc, v_sc) = refs
    qi = pl.program_id(2)
    qn = qn_ref[...] * (ATTN_SCALE * LOG2E)
    kn = kn_ref[...]

    @pl.when(qi == 0)
    def _():
        k_sc[0:CTX_LEN, :] = _head_rmsnorm(kc_ref[...], kn).astype(BF16)
        k_sc[CTX_LEN:, :] = _rope(_head_rmsnorm(kl_ref[...], kn),
                                  cos_k[...], sa_k[...], sb_k[...]).astype(BF16)
        v_sc[0:CTX_LEN, 0:HEAD_DIM] = vc_ref[...].astype(BF16)
        v_sc[CTX_LEN:, 0:HEAD_DIM] = vl_ref[...].astype(BF16)
        v_sc[:, HEAD_DIM:] = jnp.ones((CTX_LEN + SEQ, HEAD_DIM), BF16)
        if with_ctx:
            kc = k_sc[0:CTX_LEN, :]
            vc = v_sc[0:CTX_LEN, 0:HEAD_DIM]
            for h in range(Q_PER_KV):
                lanes = slice(h * HEAD_DIM, (h + 1) * HEAD_DIM)
                qh = _head_rmsnorm(qc_ref[:, lanes], qn).astype(BF16)
                o = _softmax_pv([_dot_nt(qh, kc)], [vc])
                oc_ref[:, lanes] = (o * _silu(gc_ref[:, lanes])).astype(oc_ref.dtype)

    k_all = k_sc[...]
    v_all = v_sc[...]
    for h in range(Q_PER_KV):
        lanes = slice(h * HEAD_DIM, (h + 1) * HEAD_DIM)
        for r in range(tq // GQA_ROWS):
            rows = slice(r * GQA_ROWS, (r + 1) * GQA_ROWS)
            qh = _rope(_head_rmsnorm(q_ref[rows, lanes], qn),
                       cos_q[rows, :], sa_q[rows, :], sb_q[rows, :]).astype(BF16)
            s = _dot_nt(qh, k_all)
            e = jnp.exp2(s - s.max(axis=-1, keepdims=True)).astype(BF16)
            o2 = jnp.dot(e, v_all, preferred_element_type=F32)
            o = o2[:, 0:HEAD_DIM] / o2[:, HEAD_DIM:HEAD_DIM + 1]
            o_ref[rows, lanes] = (o * _silu(g_ref[rows, lanes])).astype(o_ref.dtype)


GQA_ROWS = 256
GQA_TQ = 1024


def _gqa_mixer(p_lat, p_ctx, ctx_cols, rope_tabs, qn3, kn3, l, with_ctx):
    tq = GQA_TQ
    ck, cv = ctx_cols["b_k"], ctx_cols["b_v"]
    nq = SEQ // tq
    qw = Q_PER_KV * HEAD_DIM
    cos, sin_a, sin_b = rope_tabs
    kern = functools.partial(_gqa_kernel, tq=tq, with_ctx=with_ctx)
    in_specs = [
        pl.BlockSpec((tq, qw), lambda b, kv, qi: (b * nq + qi, COL_B_Q // Q_PER_KV + kv)),
        pl.BlockSpec((SEQ, HEAD_DIM), lambda b, kv, qi: (b, COL_B_K + kv)),
        pl.BlockSpec((SEQ, HEAD_DIM), lambda b, kv, qi: (b, COL_B_V + kv)),
        pl.BlockSpec((CTX_LEN, HEAD_DIM), lambda b, kv, qi: (b, ck + kv)),
        pl.BlockSpec((CTX_LEN, HEAD_DIM), lambda b, kv, qi: (b, cv + kv)),
        pl.BlockSpec((tq, qw), lambda b, kv, qi: (b * nq + qi, COL_B_GATE // Q_PER_KV + kv)),
    ]
    args = [p_lat, p_lat, p_lat, p_ctx, p_ctx, p_lat]
    if with_ctx:
        in_specs += [
            pl.BlockSpec((CTX_LEN, qw), lambda b, kv, qi: (b, COL_B_Q // Q_PER_KV + kv)),
            pl.BlockSpec((CTX_LEN, qw), lambda b, kv, qi: (b, COL_B_GATE // Q_PER_KV + kv)),
        ]
        args += [p_ctx, p_ctx]
    tab_q = pl.BlockSpec((tq, HEAD_DIM), lambda b, kv, qi: (qi, 0))
    tab_k = pl.BlockSpec((SEQ, HEAD_DIM), lambda b, kv, qi: (0, 0))
    gvec = pl.BlockSpec((None, 1, HEAD_DIM), lambda b, kv, qi: (l, 0, 0))
    in_specs += [tab_q, tab_q, tab_q, tab_k, tab_k, tab_k, gvec, gvec]
    args += [cos, sin_a, sin_b, cos, sin_a, sin_b, qn3, kn3]
    out_specs = [pl.BlockSpec((tq, qw), lambda b, kv, qi: (b * nq + qi, kv))]
    out_shape = [jax.ShapeDtypeStruct((BATCH * SEQ, GROUP_W), BF16)]
    if with_ctx:
        out_specs.append(pl.BlockSpec((CTX_LEN, qw), lambda b, kv, qi: (b, kv)))
        out_shape.append(jax.ShapeDtypeStruct((BATCH * CTX_LEN, GROUP_W), BF16))
    res = pl.pallas_call(
        kern,
        grid=(BATCH, KV_HEADS, nq),
        in_specs=in_specs,
        out_specs=out_specs,
        out_shape=out_shape,
        scratch_shapes=[pltpu.VMEM((CTX_LEN + SEQ, HEAD_DIM), BF16),
                        pltpu.VMEM((CTX_LEN + SEQ, 2 * HEAD_DIM), BF16)],
        compiler_params=_params(("arbitrary", "arbitrary", "arbitrary")),
        name="gqa_mixer",
    )(*args)
    return (res[0], res[1]) if with_ctx else (res[0], None)


def _log_sigmoid(v):
    return -(jnp.maximum(-v, 0.0) + jnp.log1p(jnp.exp(-jnp.abs(v))))


RET_HEADS = 2
RET_UNROLL = 8


def _ret_kernel(*refs, with_ctx):
    if with_ctx:
        (ql_ref, kl_ref, vl_ref, gl_ref, qc_ref, kc_ref, vc_ref, gc_ref,
         cos_ref, sa_ref, sb_ref, decf_ref, decb_ref, gng_ref, gnb_ref,
         o_ref, oc_ref, acc_sc, of_sc, ob_sc, qf_sc, qb_sc, kv_sc, cst_sc) = refs
    else:
        (ql_ref, kl_ref, vl_ref, gl_ref, kc_ref, vc_ref,
         cos_ref, sa_ref, sb_ref, decf_ref, decb_ref, gng_ref, gnb_ref,
         o_ref, acc_sc, of_sc, ob_sc, qf_sc, qb_sc, kv_sc, cst_sc) = refs
        qc_ref = None

    n_ctx = CTX_LEN // CHUNK
    n_lat = SEQ // CHUNK
    n_all = n_ctx + n_lat
    ii = lax.broadcasted_iota(jnp.int32, (CHUNK, CHUNK), 0).astype(F32)
    jj = lax.broadcasted_iota(jnp.int32, (CHUNK, CHUNK), 1).astype(F32)
    dfw = ii - jj
    jrow = jj[0:1, :]
    cdec = []
    kdec_rows = []
    for hh in range(RET_HEADS):
        lg_f = _log_sigmoid(decf_ref[hh])
        lg_b = _log_sigmoid(decb_ref[hh])
        base = 4 * hh
        cst_sc[base + 0] = jnp.where(dfw >= 0, jnp.exp(jnp.maximum(dfw, 0.0) * lg_f), 0.0)
        cst_sc[base + 1] = jnp.where(dfw <= 0, jnp.exp(jnp.maximum(-dfw, 0.0) * lg_b), 0.0)
        cst_sc[base + 2] = jnp.exp((ii + 1.0) * lg_f)
        cst_sc[base + 3] = jnp.exp((CHUNK - ii) * lg_b)
        kdec_rows += [jnp.exp((CHUNK - 1.0 - jrow) * lg_f), jnp.exp(jrow * lg_b)]
        cdec += [jnp.exp(CHUNK * lg_f), jnp.exp(CHUNK * lg_b)]

    def intra(c, lat, need_out):
        rows = pl.ds(pl.multiple_of(c * CHUNK, CHUNK), CHUNK)
        src = pl.ds(pl.multiple_of((c - n_ctx) * CHUNK, CHUNK), CHUNK) if lat else rows
        if lat:
            cos, sa, sb = cos_ref[src, :], sa_ref[src, :], sb_ref[src, :]
        for hh in range(RET_HEADS):
            lanes = slice(hh * HEAD_DIM, (hh + 1) * HEAD_DIM)
            base = 4 * hh
            if lat:
                kc = _rope(kl_ref[src, lanes], cos, sa, sb) * ATTN_SCALE
                vc = vl_ref[src, lanes].astype(BF16)
            else:
                kc = kc_ref[src, lanes] * ATTN_SCALE
                vc = vc_ref[src, lanes].astype(BF16)
            if need_out:
                qc = _rope(ql_ref[src, lanes], cos, sa, sb) if lat else qc_ref[src, lanes]
                a = _dot_nt(qc.astype(BF16), kc.astype(BF16))
                att = jnp.concatenate([a * cst_sc[base], a * cst_sc[base + 1]], axis=0)
                o2 = jnp.dot(att.astype(BF16), vc, preferred_element_type=F32)
                acc_sc[hh, rows, :] = o2[:CHUNK] + o2[CHUNK:]
                qf_sc[hh, rows, :] = (qc * cst_sc[base + 2]).astype(BF16)
                qb_sc[hh, rows, :] = (qc * cst_sc[base + 3]).astype(BF16)
            kt = jnp.transpose(kc)
            kd2 = jnp.concatenate([kt * kdec_rows[2 * hh], kt * kdec_rows[2 * hh + 1]], axis=0)
            kv_sc[hh, c] = jnp.dot(kd2.astype(BF16), vc, preferred_element_type=F32)

    for c in range(n_ctx):
        intra(c, False, with_ctx)

    def intra_lat(c, carry):
        intra(c, True, True)
        return carry

    lax.fori_loop(n_ctx, n_all, intra_lat, 0, unroll=RET_UNROLL)

    def inter(hh, direction, c, s, q_ref_, out_ref, need_out):
        if need_out:
            rows = pl.ds(pl.multiple_of(c * CHUNK, CHUNK), CHUNK)
            out_ref[hh, rows, :] = jnp.dot(q_ref_[hh, rows, :], s.astype(BF16),
                                           preferred_element_type=F32)
        kv = kv_sc[hh, c, direction * CHUNK:(direction + 1) * CHUNK, :]
        return s * cdec[2 * hh + direction] + kv

    states = []
    for hh in range(RET_HEADS):
        sf = jnp.zeros((HEAD_DIM, HEAD_DIM), F32)
        sb = jnp.zeros((HEAD_DIM, HEAD_DIM), F32)
        for c in range(n_ctx):
            sf = inter(hh, 0, c, sf, qf_sc, of_sc, with_ctx)
            sb = inter(hh, 1, n_ctx - 1 - c, sb, qb_sc, ob_sc, with_ctx)
        states += [sf, sb]

    def lat_step(t, carry):
        new = []
        for hh in range(RET_HEADS):
            new.append(inter(hh, 0, n_ctx + t, carry[2 * hh], qf_sc, of_sc, True))
            new.append(inter(hh, 1, n_all - 1 - t, carry[2 * hh + 1], qb_sc, ob_sc, True))
        return tuple(new)

    lax.fori_loop(0, n_lat, lat_step, tuple(states), unroll=RET_UNROLL)

    def group_norm(o, gate, lanes):
        mu = jnp.mean(o, axis=-1, keepdims=True)
        var = jnp.mean(jnp.square(o - mu), axis=-1, keepdims=True)
        on = (o - mu) * lax.rsqrt(var + LN_EPS)
        return (on * gng_ref[:, lanes] + gnb_ref[:, lanes]) * _silu(gate)

    for hh in range(RET_HEADS):
        lanes = slice(hh * HEAD_DIM, (hh + 1) * HEAD_DIM)
        o = acc_sc[hh, CTX_LEN:, :] + of_sc[hh, CTX_LEN:, :] + ob_sc[hh, CTX_LEN:, :]
        o_ref[:, lanes] = group_norm(o, gl_ref[:, lanes], lanes).astype(o_ref.dtype)
        if with_ctx:
            oc = acc_sc[hh, 0:CTX_LEN, :] + of_sc[hh, 0:CTX_LEN, :] + ob_sc[hh, 0:CTX_LEN, :]
            oc_ref[:, lanes] = group_norm(oc, gc_ref[:, lanes], lanes).astype(oc_ref.dtype)


def _retention_mixer(p_lat, p_ctx, ctx_cols, rope_tabs, dec_f, dec_b, gn_g3, gn_b3, l, with_ctx):
    cos, sin_a, sin_b = rope_tabs
    kern = functools.partial(_ret_kernel, with_ctx=with_ctx)
    width = RET_HEADS * HEAD_DIM

    def col(c0, rows):
        return pl.BlockSpec((rows, width), lambda b, h: (b, c0 // RET_HEADS + h))

    in_specs = [col(COL_C_Q, SEQ), col(COL_C_K, SEQ), col(COL_C_V, SEQ), col(COL_C_GATE, SEQ)]
    args = [p_lat, p_lat, p_lat, p_lat]
    if with_ctx:
        in_specs += [col(COL_C_Q, CTX_LEN), col(COL_C_K, CTX_LEN), col(COL_C_V, CTX_LEN),
                     col(COL_C_GATE, CTX_LEN)]
        args += [p_ctx, p_ctx, p_ctx, p_ctx]
    else:
        in_specs += [col(ctx_cols["c_k"], CTX_LEN), col(ctx_cols["c_v"], CTX_LEN)]
        args += [p_ctx, p_ctx]
    tab = pl.BlockSpec((SEQ, HEAD_DIM), lambda b, h: (0, 0))
    dec = pl.BlockSpec((None, RET_HEADS, 1, HEAD_DIM), lambda b, h: (l, h, 0, 0))
    gvec = pl.BlockSpec((None, 1, width), lambda b, h: (l, 0, h))
    in_specs += [tab, tab, tab, dec, dec, gvec, gvec]
    args += [cos, sin_a, sin_b, dec_f, dec_b, gn_g3, gn_b3]
    out_specs = [pl.BlockSpec((SEQ, width), lambda b, h: (b, h))]
    out_shape = [jax.ShapeDtypeStruct((BATCH * SEQ, GROUP_W), BF16)]
    if with_ctx:
        out_specs.append(pl.BlockSpec((CTX_LEN, width), lambda b, h: (b, h)))
        out_shape.append(jax.ShapeDtypeStruct((BATCH * CTX_LEN, GROUP_W), BF16))
    n_rows = CTX_LEN + SEQ
    rows3 = (RET_HEADS, n_rows, HEAD_DIM)
    res = pl.pallas_call(
        kern,
        grid=(BATCH, N_HEADS // RET_HEADS),
        in_specs=in_specs,
        out_specs=out_specs,
        out_shape=out_shape,
        scratch_shapes=[pltpu.VMEM(rows3, F32), pltpu.VMEM(rows3, F32), pltpu.VMEM(rows3, F32),
                        pltpu.VMEM(rows3, BF16), pltpu.VMEM(rows3, BF16),
                        pltpu.VMEM((RET_HEADS, n_rows // CHUNK, 2 * CHUNK, HEAD_DIM), F32),
                        pltpu.VMEM((RET_HEADS * 4, CHUNK, CHUNK), F32)],
        compiler_params=_params(("arbitrary", "arbitrary")),
        name="retention_mixer",
    )(*args)
    return (res[0], res[1]) if with_ctx else (res[0], None)


def _na_row_start(r):
    return min(max(r - NA_WIN_R // 2, 0), GRID_H - NA_WIN_R)


def _na_key_start(blk):
    return min(max(blk * NA_BLK_ROWS - NA_WIN_R // 2, 0), GRID_H - NA_BLK_KROWS)


def _na_table_kernel(tab_ref, bias_ref, mask_ref):
    q = lax.broadcasted_iota(jnp.int32, (GRID_W, HEAD_DIM), 0)
    lane = lax.broadcasted_iota(jnp.int32, (GRID_W, HEAD_DIM), 1)
    kcol = lane % GRID_W
    lower_half = lane < GRID_W
    off_c = jnp.clip(kcol - q, -(NA_WIN_C - 1), NA_WIN_C - 1) + (NA_WIN_C - 1)
    c_start = jnp.clip(q - NA_WIN_C // 2, 0, GRID_W - NA_WIN_C)
    col_win = jnp.where((kcol >= c_start) & (kcol < c_start + NA_WIN_C), 1.0, 0.0)
    zeros = jnp.zeros((GRID_W, HEAD_DIM), F32)
    planes = []
    for o in range(2 * NA_WIN_R - 1):
        g = zeros
        for c in range(2 * NA_WIN_C - 1):
            g = jnp.where(off_c == c, tab_ref[0, o, c] * LOG2E, g)
        planes.append(g)
    n_blk = GRID_H // NA_BLK_ROWS
    for pat, blk in enumerate((0, 1, n_blk - 1)):
        k0 = _na_key_start(blk)
        for rq in range(NA_BLK_ROWS):
            r = blk * NA_BLK_ROWS + rq
            rs = _na_row_start(r)
            for jp in range(NA_BLK_KROWS // 2):
                b_half, m_half = [], []
                for key_row in (k0 + 2 * jp, k0 + 2 * jp + 1):
                    valid = rs <= key_row < rs + NA_WIN_R
                    b_half.append(planes[key_row - r + NA_WIN_R - 1] if valid else zeros)
                    m_half.append(col_win if valid else zeros)
                rows = slice(rq * GRID_W, (rq + 1) * GRID_W)
                lanes = slice(jp * HEAD_DIM, (jp + 1) * HEAD_DIM)
                bias_ref[0, pat, rows, lanes] = jnp.where(lower_half, b_half[0], b_half[1])
                mask_ref[0, pat, rows, lanes] = jnp.where(lower_half, m_half[0], m_half[1])


def _na_tables(na_bias_all):
    n = na_bias_all.shape[0]
    spec = pl.BlockSpec((1, NA_PATTERNS, NA_BLK_Q, NA_BLK_KEYS), lambda h: (h, 0, 0, 0))
    shape = jax.ShapeDtypeStruct((n, NA_PATTERNS, NA_BLK_Q, NA_BLK_KEYS), F32)
    return pl.pallas_call(
        _na_table_kernel,
        grid=(n,),
        in_specs=[pl.BlockSpec((1, 2 * NA_WIN_R - 1, 2 * NA_WIN_C - 1), lambda h: (h, 0, 0),
                               memory_space=pltpu.SMEM)],
        out_specs=[spec, spec],
        out_shape=[shape, shape],
        compiler_params=_params(("arbitrary",)),
        name="na_tables",
    )(na_bias_all)


def _na_kernel(*refs, with_ctx):
    if with_ctx:
        (q_ref, k_ref, v_ref, g_ref, qc_ref, kc_ref, vc_ref, gc_ref, bias_ref, mask_ref,
         o_ref, oc_ref, q_sc, k_sc, v_sc, kc_sc, vc_sc, scx_sc) = refs
    else:
        (q_ref, k_ref, v_ref, g_ref, kc_ref, vc_ref, bias_ref, mask_ref,
         o_ref, q_sc, k_sc, v_sc, kc_sc, vc_sc, scx_sc) = refs
    scale = ATTN_SCALE * LOG2E
    for hh in range(NA_HEADS):
        lanes = slice(hh * HEAD_DIM, (hh + 1) * HEAD_DIM)
        q_sc[hh] = (q_ref[:, lanes] * scale).astype(BF16)
        k_sc[hh] = k_ref[:, lanes].astype(BF16)
        kc_sc[hh] = kc_ref[:, lanes].astype(BF16)
        v_sc[hh, :, 0:HEAD_DIM] = v_ref[:, lanes].astype(BF16)
        v_sc[hh, :, HEAD_DIM:] = jnp.ones((SEQ, HEAD_DIM), BF16)
        vc_sc[hh, :, 0:HEAD_DIM] = vc_ref[:, lanes].astype(BF16)
        vc_sc[hh, :, HEAD_DIM:] = jnp.ones((CTX_LEN, HEAD_DIM), BF16)
        scx_sc[hh] = _dot_nt(q_sc[hh], kc_sc[hh])
    n_blk = GRID_H // NA_BLK_ROWS

    def blk_step(blk, carry):
        k0 = jnp.clip(blk * NA_BLK_ROWS - NA_WIN_R // 2, 0, GRID_H - NA_BLK_KROWS)
        pat = jnp.where(blk == 0, 0, jnp.where(blk == n_blk - 1, 2, 1))
        rows = pl.ds(pl.multiple_of(blk * NA_BLK_Q, NA_BLK_Q), NA_BLK_Q)
        keys = pl.ds(pl.multiple_of(k0 * GRID_W, GRID_W), NA_BLK_KEYS)
        in_win = mask_ref[0, pat] > 0.5
        for hh in range(NA_HEADS):
            lanes = slice(hh * HEAD_DIM, (hh + 1) * HEAD_DIM)
            s_nb = _dot_nt(q_sc[hh, rows, :], k_sc[hh, keys, :]) + bias_ref[hh, pat]
            s_nb = jnp.where(in_win, s_nb, MASK_VALUE)
            s_cx = scx_sc[hh, rows, :]
            m = jnp.maximum(s_nb.max(axis=-1, keepdims=True), s_cx.max(axis=-1, keepdims=True))
            o2 = jnp.dot(jnp.exp2(s_nb - m).astype(BF16), v_sc[hh, keys, :],
                         preferred_element_type=F32)
            o2 = o2 + jnp.dot(jnp.exp2(s_cx - m).astype(BF16), vc_sc[hh],
                              preferred_element_type=F32)
            o = o2[:, 0:HEAD_DIM] / o2[:, HEAD_DIM:HEAD_DIM + 1]
            o_ref[rows, lanes] = (o * _silu(g_ref[rows, lanes])).astype(o_ref.dtype)
        return carry

    lax.fori_loop(0, n_blk, blk_step, 0, unroll=NA_UNROLL)

    if with_ctx:
        for hh in range(NA_HEADS):
            lanes = slice(hh * HEAD_DIM, (hh + 1) * HEAD_DIM)
            s = _dot_nt((qc_ref[:, lanes] * scale).astype(BF16), kc_sc[hh])
            o = _softmax_pv([s], [vc_sc[hh, :, 0:HEAD_DIM]])
            oc_ref[:, lanes] = (o * _silu(gc_ref[:, lanes])).astype(oc_ref.dtype)


def _na_mixer(p_lat, p_ctx, ctx_cols, na_tabs, l, with_ctx):
    kern = functools.partial(_na_kernel, with_ctx=with_ctx)
    width = NA_HEADS * HEAD_DIM
    tab0 = l * (N_HEADS // NA_HEADS)

    def col(c0, rows):
        return pl.BlockSpec((rows, width), lambda b, h: (b, c0 // NA_HEADS + h))

    in_specs = [col(COL_D_Q, SEQ), col(COL_D_K, SEQ), col(COL_D_V, SEQ), col(COL_D_GATE, SEQ)]
    args = [p_lat, p_lat, p_lat, p_lat]
    if with_ctx:
        in_specs += [col(COL_D_Q, CTX_LEN), col(COL_D_K, CTX_LEN), col(COL_D_V, CTX_LEN),
                     col(COL_D_GATE, CTX_LEN)]
        args += [p_ctx, p_ctx, p_ctx, p_ctx]
    else:
        in_specs += [col(ctx_cols["d_k"], CTX_LEN), col(ctx_cols["d_v"], CTX_LEN)]
        args += [p_ctx, p_ctx]
    tab_shape = (NA_PATTERNS, NA_BLK_Q, NA_BLK_KEYS)
    in_specs += [pl.BlockSpec((NA_HEADS,) + tab_shape, lambda b, h: (tab0 + h, 0, 0, 0)),
                 pl.BlockSpec((1,) + tab_shape, lambda b, h: (0, 0, 0, 0))]
    args += list(na_tabs)
    out_specs = [pl.BlockSpec((SEQ, width), lambda b, h: (b, h))]
    out_shape = [jax.ShapeDtypeStruct((BATCH * SEQ, GROUP_W), BF16)]
    if with_ctx:
        out_specs.append(pl.BlockSpec((CTX_LEN, width), lambda b, h: (b, h)))
        out_shape.append(jax.ShapeDtypeStruct((BATCH * CTX_LEN, GROUP_W), BF16))
    res = pl.pallas_call(
        kern,
        grid=(BATCH, N_HEADS // NA_HEADS),
        in_specs=in_specs,
        out_specs=out_specs,
        out_shape=out_shape,
        scratch_shapes=[pltpu.VMEM((NA_HEADS, SEQ, HEAD_DIM), BF16),
                        pltpu.VMEM((NA_HEADS, SEQ, HEAD_DIM), BF16),
                        pltpu.VMEM((NA_HEADS, SEQ, 2 * HEAD_DIM), BF16),
                        pltpu.VMEM((NA_HEADS, CTX_LEN, HEAD_DIM), BF16),
                        pltpu.VMEM((NA_HEADS, CTX_LEN, 2 * HEAD_DIM), BF16),
                        pltpu.VMEM((NA_HEADS, SEQ, CTX_LEN), F32)],
        compiler_params=_params(("arbitrary", "arbitrary")),
        name="na_mixer",
    )(*args)
    return (res[0], res[1]) if with_ctx else (res[0], None)


def _rope_tables():
    t = jnp.arange(SEQ)
    row = (t // GRID_W).astype(F32)
    col = (t % GRID_W).astype(F32)
    half = HEAD_DIM // 2
    inv_freq = ROPE_THETA ** (-jnp.arange(0, half, 2, dtype=F32) / half)
    ang_r = row[:, None] * inv_freq[None, :]
    ang_c = col[:, None] * inv_freq[None, :]
    ang = jnp.concatenate([ang_r, ang_r, ang_c, ang_c], axis=-1)
    cos, sin = jnp.cos(ang), jnp.sin(ang)
    quarter = (jnp.arange(HEAD_DIM) // (HEAD_DIM // 4)) % 2
    sin_a = jnp.where(quarter[None, :] == 0, -sin, 0.0)
    sin_b = jnp.where(quarter[None, :] == 1, sin, 0.0)
    return cos, sin_a, sin_b


def kernel(x, c, ctx, c_ctx, ada_w, ada_b, norm_g, w_in, conv_w, conv_b, conv_ln_g, conv_ln_b,
           gqa_qn_g, gqa_kn_g, ret_decay_fwd, ret_decay_bwd, ret_gn_g, ret_gn_b, na_bias,
           w_out, final_g):
    rope_tabs = _rope_tables()
    x_lat = x.reshape(BATCH * SEQ, D_MODEL)
    x_ctx = ctx.reshape(BATCH * CTX_LEN, D_MODEL)
    c8 = jnp.concatenate(
        [c, c_ctx[None, :], jnp.zeros((MOD_ROWS - BATCH - 1, D_MODEL), F32)], axis=0)

    def vec3(a):
        return a.reshape(DEPTH, 1, a.shape[-1])

    ada_b3, norm_g3 = vec3(ada_b), vec3(norm_g)
    conv_b3, ln_g3, ln_b3 = vec3(conv_b), vec3(conv_ln_g), vec3(conv_ln_b)
    qn3, kn3 = vec3(gqa_qn_g), vec3(gqa_kn_g)
    gn_g3, gn_b3 = vec3(ret_gn_g), vec3(ret_gn_b)
    dec_f = jnp.broadcast_to(ret_decay_fwd[:, :, None, None], (DEPTH, N_HEADS, 1, HEAD_DIM))
    dec_b = jnp.broadcast_to(ret_decay_bwd[:, :, None, None], (DEPTH, N_HEADS, 1, HEAD_DIM))

    na_tabs = _na_tables(na_bias.reshape((DEPTH * N_HEADS,) + na_bias.shape[2:]))

    for l in range(DEPTH):
        with_ctx = l < DEPTH - 1
        mod = _ada_table(c8, ada_w, ada_b3, l)
        h_lat = _norm_modulate(x_lat, norm_g3, l, mod, SEQ, None)
        h_ctx = _norm_modulate(x_ctx, norm_g3, l, mod, None, MOD_CTX_ROW)
        p_lat = _in_proj(h_lat, w_in, l)
        p_ctx = _in_proj(h_ctx, w_in, l, kv_only=not with_ctx)
        ctx_cols = CTX_COLS_FULL if with_ctx else CTX_COLS_KV_ONLY

        ya_l = _conv_mixer(p_lat, SEQ, conv_w, conv_b3, ln_g3, ln_b3, l)
        yb_l, yb_c = _gqa_mixer(p_lat, p_ctx, ctx_cols, rope_tabs, qn3, kn3, l, with_ctx)
        yc_l, yc_c = _retention_mixer(p_lat, p_ctx, ctx_cols, rope_tabs, dec_f, dec_b, gn_g3, gn_b3,
                                      l, with_ctx)
        yd_l, yd_c = _na_mixer(p_lat, p_ctx, ctx_cols, na_tabs, l, with_ctx)

        x_new = _out_proj((ya_l, yb_l, yc_l, yd_l), w_out, l, x_lat, mod, SEQ, None)
        if with_ctx:
            ya_c = _conv_mixer(p_ctx, CTX_LEN, conv_w, conv_b3, ln_g3, ln_b3, l)
            x_ctx = _out_proj((ya_c, yb_c, yc_c, yd_c), w_out, l, x_ctx, mod, None, MOD_CTX_ROW)
        x_lat = x_new

    out = _final_norm(x_lat, final_g.reshape(1, D_MODEL))
    return out.reshape(BATCH, SEQ, D_MODEL)
```

```python
import functools

import jax
import jax.numpy as jnp
from jax import lax
from jax.experimental import pallas as pl
from jax.experimental.pallas import tpu as pltpu

F32 = jnp.float32
BF16 = jnp.bfloat16

D_MODEL = 4096
BATCH = 4
SEQ = 2048
DEPTH = 2
GRID_W = 64
GRID_H = SEQ // GRID_W
CTX_LEN = 256
GROUP_W = 1024
HEAD_DIM = 128
N_HEADS = GROUP_W // HEAD_DIM
KV_HEADS = 2
Q_PER_KV = N_HEADS // KV_HEADS
CONV_WIDTH = 31
CONV_HALF = CONV_WIDTH // 2
CHUNK = 128
NA_WIN_R = 8
NA_WIN_C = 16
ROPE_THETA = 10000.0
NORM_EPS = 1e-6
LN_EPS = 1e-5
ATTN_SCALE = HEAD_DIM ** -0.5
MASK_VALUE = -1e30
P_TOTAL = 13 * GROUP_W + 2 * KV_HEADS * HEAD_DIM

COL_A_GLU = 0
COL_A_GATE = 16
COL_B_Q = 24
COL_B_K = 32
COL_B_V = 34
COL_B_GATE = 36
COL_C_Q = 44
COL_C_K = 52
COL_C_V = 60
COL_C_GATE = 68
COL_D_Q = 76
COL_D_K = 84
COL_D_V = 92
COL_D_GATE = 100

NORM_TM = 512
MOD_ROWS = 8
MOD_CTX_ROW = BATCH
VMEM_LIMIT = 56 * 1024 * 1024

NA_BLK_ROWS = 4
NA_BLK_KROWS = 12
NA_BLK_Q = NA_BLK_ROWS * GRID_W
NA_BLK_KEYS = NA_BLK_KROWS * GRID_W
NA_PATTERNS = 3
NA_UNROLL = 4
NA_HEADS = 2


def _params(sem, flags=None):
    return pltpu.CompilerParams(dimension_semantics=sem, vmem_limit_bytes=VMEM_LIMIT, flags=flags)


def _silu(v):
    return v * jax.nn.sigmoid(v)


def _mod_kernel(c_ref, w_ref, b_ref, o_ref):
    a = _silu(c_ref[...]).astype(BF16)
    o_ref[...] = jnp.dot(a, w_ref[...].astype(BF16), preferred_element_type=F32) + b_ref[...]


def _ada_table(c8, ada_w, ada_b3, l):
    tn = 512
    n = 3 * D_MODEL
    return pl.pallas_call(
        _mod_kernel,
        grid=(n // tn,),
        in_specs=[
            pl.BlockSpec((MOD_ROWS, D_MODEL), lambda j: (0, 0)),
            pl.BlockSpec((None, D_MODEL, tn), lambda j: (l, 0, j)),
            pl.BlockSpec((None, 1, tn), lambda j: (l, 0, j)),
        ],
        out_specs=pl.BlockSpec((MOD_ROWS, tn), lambda j: (0, j)),
        out_shape=jax.ShapeDtypeStruct((MOD_ROWS, n), F32),
        compiler_params=_params(("arbitrary",)),
        name="ada_table",
    )(c8, ada_w, ada_b3)


def _normmod_kernel(x_ref, g_ref, shift_ref, scale_ref, o_ref, *, tiles_per_row, fixed_row):
    if fixed_row is None:
        r = pl.program_id(0) // tiles_per_row
    else:
        r = fixed_row
    x = x_ref[...]
    y = x * lax.rsqrt(jnp.mean(x * x, axis=-1, keepdims=True) + NORM_EPS)
    y = y * g_ref[...]
    shift = shift_ref[pl.ds(r, 1), :]
    scale = scale_ref[pl.ds(r, 1), :]
    o_ref[...] = (y * (1.0 + scale) + shift).astype(o_ref.dtype)


def _norm_modulate(x2, norm_g3, l, mod, rows_per_mod, fixed_row):
    m = x2.shape[0]
    tm = NORM_TM
    tiles_per_row = None if rows_per_mod is None else rows_per_mod // tm
    kern = functools.partial(_normmod_kernel, tiles_per_row=tiles_per_row, fixed_row=fixed_row)
    return pl.pallas_call(
        kern,
        grid=(m // tm,),
        in_specs=[
            pl.BlockSpec((tm, D_MODEL), lambda i: (i, 0)),
            pl.BlockSpec((None, 1, D_MODEL), lambda i: (l, 0, 0)),
            pl.BlockSpec((MOD_ROWS, D_MODEL), lambda i: (0, 0)),
            pl.BlockSpec((MOD_ROWS, D_MODEL), lambda i: (0, 1)),
        ],
        out_specs=pl.BlockSpec((tm, D_MODEL), lambda i: (i, 0)),
        out_shape=jax.ShapeDtypeStruct((m, D_MODEL), BF16),
        compiler_params=_params(("arbitrary",)),
        name="norm_modulate",
    )(x2, norm_g3, mod, mod)


PROJ_TM = 1024
PROJ_TN = 512
IN_PROJ_TN = PROJ_TN


def _stage_weight_chunk(w_ref, wbf_ref, n_cols):
    j, i = pl.program_id(0), pl.program_id(1)
    chunk = w_ref.shape[0]

    @pl.when(j < n_cols)
    def _():
        wbf_ref[j % 2, pl.ds(pl.multiple_of(i * chunk, chunk), chunk), :] = w_ref[...].astype(BF16)

    return j, (j + 1) % 2


def _mm_kernel(a_ref, w_ref, o_ref, wbf_ref, *, n_cols):
    j, slot = _stage_weight_chunk(w_ref, wbf_ref, n_cols)

    @pl.when(j > 0)
    def _():
        o_ref[...] = jnp.dot(a_ref[...], wbf_ref[slot], preferred_element_type=F32)


def _ctx_kv_col_tile(j):
    b_kv = COL_B_K * HEAD_DIM // PROJ_TN
    c_kv = COL_C_K * HEAD_DIM // PROJ_TN
    d_kv = COL_D_K * HEAD_DIM // PROJ_TN
    n_c = 2 * GROUP_W // PROJ_TN
    return jnp.where(j < 1, b_kv, jnp.where(j < 1 + n_c, c_kv + (j - 1), d_kv + (j - 1 - n_c)))


CTX_KV_TILES = 1 + 2 * (2 * GROUP_W // PROJ_TN)
CTX_COLS_FULL = dict(b_k=COL_B_K, b_v=COL_B_V, c_k=COL_C_K, c_v=COL_C_V, d_k=COL_D_K, d_v=COL_D_V)
CTX_COLS_KV_ONLY = dict(b_k=0, b_v=KV_HEADS, c_k=2 * KV_HEADS, c_v=2 * KV_HEADS + N_HEADS,
                        d_k=2 * KV_HEADS + 2 * N_HEADS, d_v=2 * KV_HEADS + 3 * N_HEADS)


def _mm_single_kernel(a_ref, w_ref, o_ref):
    o_ref[...] = jnp.dot(a_ref[...], w_ref[...].astype(BF16), preferred_element_type=F32)


def _in_proj_single(h, w_in, l, kv_only):
    tm, tn = PROJ_TM, PROJ_TN
    n_tiles = CTX_KV_TILES if kv_only else P_TOTAL // tn
    col = _ctx_kv_col_tile if kv_only else (lambda j: j)
    return pl.pallas_call(
        _mm_single_kernel,
        grid=(n_tiles,),
        in_specs=[
            pl.BlockSpec((tm, D_MODEL), lambda j: (0, 0)),
            pl.BlockSpec((None, D_MODEL, tn), lambda j: (l, 0, col(j))),
        ],
        out_specs=pl.BlockSpec((tm, tn), lambda j: (0, j)),
        out_shape=jax.ShapeDtypeStruct((tm, n_tiles * tn), F32),
        compiler_params=_params(("arbitrary",)),
        name="in_proj_ctx",
    )(h, w_in)


def _in_proj(h, w_in, l, kv_only=False):
    m = h.shape[0]
    if m == PROJ_TM:
        return _in_proj_single(h, w_in, l, kv_only)
    tm, tn = PROJ_TM, IN_PROJ_TN
    n_tiles = P_TOTAL // tn
    n_rows = m // tm
    chunk = D_MODEL // n_rows
    return pl.pallas_call(
        functools.partial(_mm_kernel, n_cols=n_tiles),
        grid=(n_tiles + 1, n_rows),
        in_specs=[
            pl.BlockSpec((tm, D_MODEL), lambda j, i: (jnp.where(j == 0, 0, i), 0)),
            pl.BlockSpec((None, chunk, tn), lambda j, i: (l, i, jnp.minimum(j, n_tiles - 1))),
        ],
        out_specs=pl.BlockSpec((tm, tn), lambda j, i: (jnp.where(j == 0, 0, i), jnp.maximum(j - 1, 0))),
        out_shape=jax.ShapeDtypeStruct((m, n_tiles * tn), F32),
        scratch_shapes=[pltpu.VMEM((2, D_MODEL, tn), BF16)],
        compiler_params=_params(("arbitrary", "arbitrary")),
        name="in_proj",
    )(h, w_in)


def _out_kernel(ya_ref, yb_ref, yc_ref, yd_ref, w_ref, x_ref, gate_ref, o_ref, wbf_ref, *,
                tiles_per_row, fixed_row, n_cols):
    if fixed_row is None:
        r = pl.program_id(1) // tiles_per_row
    else:
        r = fixed_row
    j, slot = _stage_weight_chunk(w_ref, wbf_ref, n_cols)

    @pl.when(j > 0)
    def _():
        acc = None
        for g, y_ref in enumerate((ya_ref, yb_ref, yc_ref, yd_ref)):
            part = jnp.dot(y_ref[...], wbf_ref[slot, g * GROUP_W:(g + 1) * GROUP_W, :],
                           preferred_element_type=F32)
            acc = part if acc is None else acc + part
        gate = gate_ref[pl.ds(r, 1), :]
        o_ref[...] = x_ref[...] + gate * acc


def _out_single_kernel(ya_ref, yb_ref, yc_ref, yd_ref, w_ref, x_ref, gate_ref, o_ref, *, fixed_row):
    acc = None
    for g, y_ref in enumerate((ya_ref, yb_ref, yc_ref, yd_ref)):
        part = jnp.dot(y_ref[...], w_ref[g * GROUP_W:(g + 1) * GROUP_W, :].astype(BF16),
                       preferred_element_type=F32)
        acc = part if acc is None else acc + part
    o_ref[...] = x_ref[...] + gate_ref[fixed_row:fixed_row + 1, :] * acc


def _out_proj_single(ys, w_out, l, x2, mod, fixed_row):
    tm, tn = PROJ_TM, PROJ_TN
    y_spec = pl.BlockSpec((tm, GROUP_W), lambda j: (0, 0))
    gate_col0 = 2 * D_MODEL // tn
    return pl.pallas_call(
        functools.partial(_out_single_kernel, fixed_row=fixed_row),
        grid=(D_MODEL // tn,),
        in_specs=[
            y_spec, y_spec, y_spec, y_spec,
            pl.BlockSpec((None, D_MODEL, tn), lambda j: (l, 0, j)),
            pl.BlockSpec((tm, tn), lambda j: (0, j)),
            pl.BlockSpec((MOD_ROWS, tn), lambda j: (0, gate_col0 + j)),
        ],
        out_specs=pl.BlockSpec((tm, tn), lambda j: (0, j)),
        out_shape=jax.ShapeDtypeStruct((tm, D_MODEL), F32),
        compiler_params=_params(("arbitrary",)),
        name="out_proj_ctx",
    )(*ys, w_out, x2, mod)


def _out_proj(ys, w_out, l, x2, mod, rows_per_mod, fixed_row):
    m = x2.shape[0]
    if m == PROJ_TM:
        return _out_proj_single(ys, w_out, l, x2, mod, fixed_row)
    tm, tn = PROJ_TM, PROJ_TN
    tiles_per_row = None if rows_per_mod is None else rows_per_mod // tm
    n_cols = D_MODEL // tn
    n_rows = m // tm
    chunk = D_MODEL // n_rows
    kern = functools.partial(_out_kernel, tiles_per_row=tiles_per_row, fixed_row=fixed_row,
                             n_cols=n_cols)

    def out_col(j):
        return jnp.maximum(j - 1, 0)

    def row(j, i):
        return jnp.where(j == 0, 0, i)

    y_spec = pl.BlockSpec((tm, GROUP_W), lambda j, i: (row(j, i), 0))
    gate_col0 = 2 * D_MODEL // tn
    return pl.pallas_call(
        kern,
        grid=(n_cols + 1, n_rows),
        in_specs=[
            y_spec, y_spec, y_spec, y_spec,
            pl.BlockSpec((None, chunk, tn), lambda j, i: (l, i, jnp.minimum(j, n_cols - 1))),
            pl.BlockSpec((tm, tn), lambda j, i: (row(j, i), out_col(j))),
            pl.BlockSpec((MOD_ROWS, tn), lambda j, i: (0, gate_col0 + out_col(j))),
        ],
        out_specs=pl.BlockSpec((tm, tn), lambda j, i: (row(j, i), out_col(j))),
        out_shape=jax.ShapeDtypeStruct((m, D_MODEL), F32),
        scratch_shapes=[pltpu.VMEM((2, D_MODEL, tn), BF16)],
        compiler_params=_params(("arbitrary", "arbitrary")),
        name="out_proj",
    )(*ys, w_out, x2, mod)


def _final_norm_kernel(x_ref, g_ref, o_ref):
    x = x_ref[...]
    o_ref[...] = x * lax.rsqrt(jnp.mean(x * x, axis=-1, keepdims=True) + NORM_EPS) * g_ref[...]


def _final_norm(x2, g2):
    m = x2.shape[0]
    tm = NORM_TM
    return pl.pallas_call(
        _final_norm_kernel,
        grid=(m // tm,),
        in_specs=[pl.BlockSpec((tm, D_MODEL), lambda i: (i, 0)),
                  pl.BlockSpec((1, D_MODEL), lambda i: (0, 0))],
        out_specs=pl.BlockSpec((tm, D_MODEL), lambda i: (i, 0)),
        out_shape=jax.ShapeDtypeStruct((m, D_MODEL), F32),
        compiler_params=_params(("arbitrary",)),
        name="final_norm",
    )(x2, g2)


CONV_HALO = 16
CONV_SUB = 64
SUBLANES = 8


def _conv_kernel(cur_ref, prev_ref, next_ref, gate_ref, w_ref, b_ref, lng_ref, lnb_ref, o_ref,
                 u_ref, ush_ref, c_ref, *, tile, n_tiles):
    t = pl.program_id(1)

    def glu(ref):
        return ref[:, :GROUP_W] * jax.nn.sigmoid(ref[:, GROUP_W:])

    u_ref[CONV_HALO:CONV_HALO + tile, :] = glu(cur_ref)
    u_ref[0:CONV_HALO, :] = jnp.where(t > 0, glu(prev_ref), 0.0)
    u_ref[CONV_HALO + tile:, :] = jnp.where(t < n_tiles - 1, glu(next_ref), 0.0)

    sh_rows = ush_ref.shape[1]
    for b in range(1, SUBLANES):
        ush_ref[b - 1] = u_ref[b:b + sh_rows, :]

    base = CONV_HALO - CONV_HALF

    for s in range(tile // CONV_SUB):
        row0 = s * CONV_SUB
        for c in range(GROUP_W // HEAD_DIM):
            lanes = slice(c * HEAD_DIM, (c + 1) * HEAD_DIM)
            acc = jnp.zeros((CONV_SUB, HEAD_DIM), F32)
            for k in range(CONV_WIDTH):
                b = (base + k) % SUBLANES
                rows = slice(row0 + base + k - b, row0 + base + k - b + CONV_SUB)
                tap = u_ref[rows, lanes] if b == 0 else ush_ref[b - 1, rows, lanes]
                acc = acc + w_ref[k:k + 1, lanes] * tap
            c_ref[row0:row0 + CONV_SUB, lanes] = acc + b_ref[:, lanes]

    v = c_ref[...]
    mu = jnp.mean(v, axis=-1, keepdims=True)
    var = jnp.mean(jnp.square(v - mu), axis=-1, keepdims=True)
    y = (v - mu) * lax.rsqrt(var + LN_EPS) * lng_ref[...] + lnb_ref[...]
    o_ref[...] = (_silu(y) * _silu(gate_ref[...])).astype(o_ref.dtype)


def _conv_mixer(p, seq_len, conv_w, conv_b3, ln_g3, ln_b3, l):
    tile = 256
    n_tiles = seq_len // tile
    n_seq = p.shape[0] // seq_len
    halo_per_tile = tile // CONV_HALO
    last_halo = p.shape[0] // CONV_HALO - 1
    kern = functools.partial(_conv_kernel, tile=tile, n_tiles=n_tiles)
    vec = pl.BlockSpec((None, 1, GROUP_W), lambda b, t: (l, 0, 0))
    return pl.pallas_call(
        kern,
        grid=(n_seq, n_tiles),
        in_specs=[
            pl.BlockSpec((tile, 2 * GROUP_W), lambda b, t: (b * n_tiles + t, 0)),
            pl.BlockSpec((CONV_HALO, 2 * GROUP_W),
                         lambda b, t: (jnp.maximum((b * n_tiles + t) * halo_per_tile - 1, 0), 0)),
            pl.BlockSpec((CONV_HALO, 2 * GROUP_W),
                         lambda b, t: (jnp.minimum((b * n_tiles + t + 1) * halo_per_tile, last_halo), 0)),
            pl.BlockSpec((tile, GROUP_W), lambda b, t: (b * n_tiles + t, COL_A_GATE // 8)),
            pl.BlockSpec((None, CONV_WIDTH, GROUP_W), lambda b, t: (l, 0, 0)),
            vec, vec, vec,
        ],
        out_specs=pl.BlockSpec((tile, GROUP_W), lambda b, t: (b * n_tiles + t, 0)),
        out_shape=jax.ShapeDtypeStruct((p.shape[0], GROUP_W), BF16),
        scratch_shapes=[pltpu.VMEM((tile + 2 * CONV_HALO, GROUP_W), F32),
                        pltpu.VMEM((SUBLANES - 1, tile + 2 * CONV_HALO - SUBLANES, GROUP_W), F32),
                        pltpu.VMEM((tile, GROUP_W), F32)],
        compiler_params=_params(("arbitrary", "arbitrary")),
        name="conv_mixer",
    )(p, p, p, p, conv_w, conv_b3, ln_g3, ln_b3)


def _head_rmsnorm(v, g):
    return v * lax.rsqrt(jnp.mean(v * v, axis=-1, keepdims=True) + NORM_EPS) * g


def _rope(v, cos, sin_a, sin_b):
    return (v * cos + pltpu.roll(v, HEAD_DIM - HEAD_DIM // 4, 1) * sin_a
            + pltpu.roll(v, HEAD_DIM // 4, 1) * sin_b)


def _dot_nt(a, b):
    return lax.dot_general(a, b, (((1,), (1,)), ((), ())), preferred_element_type=F32)


LOG2E = 1.4426950408889634


def _softmax_pv(s_list, v_list):
    m = s_list[0].max(axis=-1, keepdims=True)
    for s in s_list[1:]:
        m = jnp.maximum(m, s.max(axis=-1, keepdims=True))
    den = None
    acc = None
    for s, v in zip(s_list, v_list):
        e = jnp.exp2(s - m)
        d = e.sum(axis=-1, keepdims=True)
        o = jnp.dot(e.astype(BF16), v, preferred_element_type=F32)
        den = d if den is None else den + d
        acc = o if acc is None else acc + o
    return acc / den


def _gqa_kernel(*refs, tq, with_ctx):
    if with_ctx:
        (q_ref, kl_ref, vl_ref, kc_ref, vc_ref, g_ref, qc_ref, gc_ref,
         cos_q, sa_q, sb_q, cos_k, sa_k, sb_k, qn_ref, kn_ref,
         o_ref, oc_ref, k_sc, v_sc) = refs
    else:
        (q_ref, kl_ref, vl_ref, kc_ref, vc_ref, g_ref,
         cos_q, sa_q, sb_q, cos_k, sa_k, sb_k, qn_ref, kn_ref,
         o_ref, k_sc, v_sc) = refs
    qi = pl.program_id(2)
    qn = qn_ref[...] * (ATTN_SCALE * LOG2E)
    kn = kn_ref[...]

    @pl.when(qi == 0)
    def _():
        k_sc[0:CTX_LEN, :] = _head_rmsnorm(kc_ref[...], kn).astype(BF16)
        k_sc[CTX_LEN:, :] = _rope(_head_rmsnorm(kl_ref[...], kn),
                                  cos_k[...], sa_k[...], sb_k[...]).astype(BF16)
        v_sc[0:CTX_LEN, 0:HEAD_DIM] = vc_ref[...].astype(BF16)
        v_sc[CTX_LEN:, 0:HEAD_DIM] = vl_ref[...].astype(BF16)
        v_sc[:, HEAD_DIM:] = jnp.ones((CTX_LEN + SEQ, HEAD_DIM), BF16)
        if with_ctx:
            kc = k_sc[0:CTX_LEN, :]
            vc = v_sc[0:CTX_LEN, 0:HEAD_DIM]
            for h in range(Q_PER_KV):
                lanes = slice(h * HEAD_DIM, (h + 1) * HEAD_DIM)
                qh = _head_rmsnorm(qc_ref[:, lanes], qn).astype(BF16)
                o = _softmax_pv([_dot_nt(qh, kc)], [vc])
                oc_ref[:, lanes] = (o * _silu(gc_ref[:, lanes])).astype(oc_ref.dtype)

    k_all = k_sc[...]
    v_all = v_sc[...]
    for h in range(Q_PER_KV):
        lanes = slice(h * HEAD_DIM, (h + 1) * HEAD_DIM)
        for r in range(tq // GQA_ROWS):
            rows = slice(r * GQA_ROWS, (r + 1) * GQA_ROWS)
            qh = _rope(_head_rmsnorm(q_ref[rows, lanes], qn),
                       cos_q[rows, :], sa_q[rows, :], sb_q[rows, :]).astype(BF16)
            s = _dot_nt(qh, k_all)
            e = jnp.exp2(s - s.max(axis=-1, keepdims=True)).astype(BF16)
            o2 = jnp.dot(e, v_all, preferred_element_type=F32)
            o = o2[:, 0:HEAD_DIM] / o2[:, HEAD_DIM:HEAD_DIM + 1]
            o_ref[rows, lanes] = (o * _silu(g_ref[rows, lanes])).astype(o_ref.dtype)


GQA_ROWS = 256
GQA_TQ = 1024


def _gqa_mixer(p_lat, p_ctx, ctx_cols, rope_tabs, qn3, kn3, l, with_ctx):
    tq = GQA_TQ
    ck, cv = ctx_cols["b_k"], ctx_cols["b_v"]
    nq = SEQ // tq
    qw = Q_PER_KV * HEAD_DIM
    cos, sin_a, sin_b = rope_tabs
    kern = functools.partial(_gqa_kernel, tq=tq, with_ctx=with_ctx)
    in_specs = [
        pl.BlockSpec((tq, qw), lambda b, kv, qi: (b * nq + qi, COL_B_Q // Q_PER_KV + kv)),
        pl.BlockSpec((SEQ, HEAD_DIM), lambda b, kv, qi: (b, COL_B_K + kv)),
        pl.BlockSpec((SEQ, HEAD_DIM), lambda b, kv, qi: (b, COL_B_V + kv)),
        pl.BlockSpec((CTX_LEN, HEAD_DIM), lambda b, kv, qi: (b, ck + kv)),
        pl.BlockSpec((CTX_LEN, HEAD_DIM), lambda b, kv, qi: (b, cv + kv)),
        pl.BlockSpec((tq, qw), lambda b, kv, qi: (b * nq + qi, COL_B_GATE // Q_PER_KV + kv)),
    ]
    args = [p_lat, p_lat, p_lat, p_ctx, p_ctx, p_lat]
    if with_ctx:
        in_specs += [
            pl.BlockSpec((CTX_LEN, qw), lambda b, kv, qi: (b, COL_B_Q // Q_PER_KV + kv)),
            pl.BlockSpec((CTX_LEN, qw), lambda b, kv, qi: (b, COL_B_GATE // Q_PER_KV + kv)),
        ]
        args += [p_ctx, p_ctx]
    tab_q = pl.BlockSpec((tq, HEAD_DIM), lambda b, kv, qi: (qi, 0))
    tab_k = pl.BlockSpec((SEQ, HEAD_DIM), lambda b, kv, qi: (0, 0))
    gvec = pl.BlockSpec((None, 1, HEAD_DIM), lambda b, kv, qi: (l, 0, 0))
    in_specs += [tab_q, tab_q, tab_q, tab_k, tab_k, tab_k, gvec, gvec]
    args += [cos, sin_a, sin_b, cos, sin_a, sin_b, qn3, kn3]
    out_specs = [pl.BlockSpec((tq, qw), lambda b, kv, qi: (b * nq + qi, kv))]
    out_shape = [jax.ShapeDtypeStruct((BATCH * SEQ, GROUP_W), BF16)]
    if with_ctx:
        out_specs.append(pl.BlockSpec((CTX_LEN, qw), lambda b, kv, qi: (b, kv)))
        out_shape.append(jax.ShapeDtypeStruct((BATCH * CTX_LEN, GROUP_W), BF16))
    res = pl.pallas_call(
        kern,
        grid=(BATCH, KV_HEADS, nq),
        in_specs=in_specs,
        out_specs=out_specs,
        out_shape=out_shape,
        scratch_shapes=[pltpu.VMEM((CTX_LEN + SEQ, HEAD_DIM), BF16),
                        pltpu.VMEM((CTX_LEN + SEQ, 2 * HEAD_DIM), BF16)],
        compiler_params=_params(("arbitrary", "arbitrary", "arbitrary")),
        name="gqa_mixer",
    )(*args)
    return (res[0], res[1]) if with_ctx else (res[0], None)


def _log_sigmoid(v):
    return -(jnp.maximum(-v, 0.0) + jnp.log1p(jnp.exp(-jnp.abs(v))))


RET_HEADS = 2
RET_UNROLL = 8


def _ret_kernel(*refs, with_ctx):
    if with_ctx:
        (ql_ref, kl_ref, vl_ref, gl_ref, qc_ref, kc_ref, vc_ref, gc_ref,
         cos_ref, sa_ref, sb_ref, decf_ref, decb_ref, gng_ref, gnb_ref,
         o_ref, oc_ref, acc_sc, of_sc, ob_sc, qf_sc, qb_sc, kv_sc, cst_sc) = refs
    else:
        (ql_ref, kl_ref, vl_ref, gl_ref, kc_ref, vc_ref,
         cos_ref, sa_ref, sb_ref, decf_ref, decb_ref, gng_ref, gnb_ref,
         o_ref, acc_sc, of_sc, ob_sc, qf_sc, qb_sc, kv_sc, cst_sc) = refs
        qc_ref = None

    n_ctx = CTX_LEN // CHUNK
    n_lat = SEQ // CHUNK
    n_all = n_ctx + n_lat
    ii = lax.broadcasted_iota(jnp.int32, (CHUNK, CHUNK), 0).astype(F32)
    jj = lax.broadcasted_iota(jnp.int32, (CHUNK, CHUNK), 1).astype(F32)
    dfw = ii - jj
    jrow = jj[0:1, :]
    cdec = []
    kdec_cols = []
    for hh in range(RET_HEADS):
        lg_f = _log_sigmoid(decf_ref[hh])
        lg_b = _log_sigmoid(decb_ref[hh])
        base = 4 * hh
        cst_sc[base + 0] = jnp.where(dfw >= 0, jnp.exp(jnp.maximum(dfw, 0.0) * lg_f), 0.0)
        cst_sc[base + 1] = jnp.where(dfw <= 0, jnp.exp(jnp.maximum(-dfw, 0.0) * lg_b), 0.0)
        cst_sc[base + 2] = jnp.exp((ii + 1.0) * lg_f)
        cst_sc[base + 3] = jnp.exp((CHUNK - ii) * lg_b)
        kdec_cols += [jnp.exp((CHUNK - 1.0 - ii) * lg_f), jnp.exp(ii * lg_b)]
        cdec += [jnp.exp(CHUNK * lg_f), jnp.exp(CHUNK * lg_b)]

    def intra(c, lat, need_out):
        rows = pl.ds(pl.multiple_of(c * CHUNK, CHUNK), CHUNK)
        src = pl.ds(pl.multiple_of((c - n_ctx) * CHUNK, CHUNK), CHUNK) if lat else rows
        if lat:
            cos, sa, sb = cos_ref[src, :], sa_ref[src, :], sb_ref[src, :]
        for hh in range(RET_HEADS):
            lanes = slice(hh * HEAD_DIM, (hh + 1) * HEAD_DIM)
            base = 4 * hh
            if lat:
                kc = _rope(kl_ref[src, lanes], cos, sa, sb) * ATTN_SCALE
                vc = vl_ref[src, lanes].astype(BF16)
            else:
                kc = kc_ref[src, lanes] * ATTN_SCALE
                vc = vc_ref[src, lanes].astype(BF16)
            if need_out:
                qc = _rope(ql_ref[src, lanes], cos, sa, sb) if lat else qc_ref[src, lanes]
                a = _dot_nt(qc.astype(BF16), kc.astype(BF16))
                att = jnp.concatenate([a * cst_sc[base], a * cst_sc[base + 1]], axis=0)
                o2 = jnp.dot(att.astype(BF16), vc, preferred_element_type=F32)
                acc_sc[hh, rows, :] = o2[:CHUNK] + o2[CHUNK:]
                qf_sc[hh, rows, :] = (qc * cst_sc[base + 2]).astype(BF16)
                qb_sc[hh, rows, :] = (qc * cst_sc[base + 3]).astype(BF16)
            for d in range(2):
                kd = (kc * kdec_cols[2 * hh + d]).astype(BF16)
                kv_sc[hh, c, d * CHUNK:(d + 1) * CHUNK, :] = lax.dot_general(
                    kd, vc, (((0,), (0,)), ((), ())), preferred_element_type=F32)

    for c in range(n_ctx):
        intra(c, False, with_ctx)

    def intra_lat(c, carry):
        intra(c, True, True)
        return carry

    lax.fori_loop(n_ctx, n_all, intra_lat, 0, unroll=RET_UNROLL)

    def inter(hh, direction, c, s, q_ref_, out_ref, need_out):
        if need_out:
            rows = pl.ds(pl.multiple_of(c * CHUNK, CHUNK), CHUNK)
            out_ref[hh, rows, :] = jnp.dot(q_ref_[hh, rows, :], s.astype(BF16),
                                           preferred_element_type=F32)
        kv = kv_sc[hh, c, direction * CHUNK:(direction + 1) * CHUNK, :]
        return s * cdec[2 * hh + direction] + kv

    states = []
    for hh in range(RET_HEADS):
        sf = jnp.zeros((HEAD_DIM, HEAD_DIM), F32)
        sb = jnp.zeros((HEAD_DIM, HEAD_DIM), F32)
        for c in range(n_ctx):
            sf = inter(hh, 0, c, sf, qf_sc, of_sc, with_ctx)
            sb = inter(hh, 1, n_ctx - 1 - c, sb, qb_sc, ob_sc, with_ctx)
        states += [sf, sb]

    def lat_step(t, carry):
        new = []
        for hh in range(RET_HEADS):
            new.append(inter(hh, 0, n_ctx + t, carry[2 * hh], qf_sc, of_sc, True))
            new.append(inter(hh, 1, n_all - 1 - t, carry[2 * hh + 1], qb_sc, ob_sc, True))
        return tuple(new)

    lax.fori_loop(0, n_lat, lat_step, tuple(states), unroll=RET_UNROLL)

    def group_norm(o, gate, lanes):
        mu = jnp.mean(o, axis=-1, keepdims=True)
        var = jnp.mean(jnp.square(o - mu), axis=-1, keepdims=True)
        on = (o - mu) * lax.rsqrt(var + LN_EPS)
        return (on * gng_ref[:, lanes] + gnb_ref[:, lanes]) * _silu(gate)

    for hh in range(RET_HEADS):
        lanes = slice(hh * HEAD_DIM, (hh + 1) * HEAD_DIM)
        o = acc_sc[hh, CTX_LEN:, :] + of_sc[hh, CTX_LEN:, :] + ob_sc[hh, CTX_LEN:, :]
        o_ref[:, lanes] = group_norm(o, gl_ref[:, lanes], lanes).astype(o_ref.dtype)
        if with_ctx:
            oc = acc_sc[hh, 0:CTX_LEN, :] + of_sc[hh, 0:CTX_LEN, :] + ob_sc[hh, 0:CTX_LEN, :]
            oc_ref[:, lanes] = group_norm(oc, gc_ref[:, lanes], lanes).astype(oc_ref.dtype)


def _retention_mixer(p_lat, p_ctx, ctx_cols, rope_tabs, dec_f, dec_b, gn_g3, gn_b3, l, with_ctx):
    cos, sin_a, sin_b = rope_tabs
    kern = functools.partial(_ret_kernel, with_ctx=with_ctx)
    width = RET_HEADS * HEAD_DIM

    def col(c0, rows):
        return pl.BlockSpec((rows, width), lambda b, h: (b, c0 // RET_HEADS + h))

    in_specs = [col(COL_C_Q, SEQ), col(COL_C_K, SEQ), col(COL_C_V, SEQ), col(COL_C_GATE, SEQ)]
    args = [p_lat, p_lat, p_lat, p_lat]
    if with_ctx:
        in_specs += [col(COL_C_Q, CTX_LEN), col(COL_C_K, CTX_LEN), col(COL_C_V, CTX_LEN),
                     col(COL_C_GATE, CTX_LEN)]
        args += [p_ctx, p_ctx, p_ctx, p_ctx]
    else:
        in_specs += [col(ctx_cols["c_k"], CTX_LEN), col(ctx_cols["c_v"], CTX_LEN)]
        args += [p_ctx, p_ctx]
    tab = pl.BlockSpec((SEQ, HEAD_DIM), lambda b, h: (0, 0))
    dec = pl.BlockSpec((None, RET_HEADS, 1, HEAD_DIM), lambda b, h: (l, h, 0, 0))
    gvec = pl.BlockSpec((None, 1, width), lambda b, h: (l, 0, h))
    in_specs += [tab, tab, tab, dec, dec, gvec, gvec]
    args += [cos, sin_a, sin_b, dec_f, dec_b, gn_g3, gn_b3]
    out_specs = [pl.BlockSpec((SEQ, width), lambda b, h: (b, h))]
    out_shape = [jax.ShapeDtypeStruct((BATCH * SEQ, GROUP_W), BF16)]
    if with_ctx:
        out_specs.append(pl.BlockSpec((CTX_LEN, width), lambda b, h: (b, h)))
        out_shape.append(jax.ShapeDtypeStruct((BATCH * CTX_LEN, GROUP_W), BF16))
    n_rows = CTX_LEN + SEQ
    rows3 = (RET_HEADS, n_rows, HEAD_DIM)
    res = pl.pallas_call(
        kern,
        grid=(BATCH, N_HEADS // RET_HEADS),
        in_specs=in_specs,
        out_specs=out_specs,
        out_shape=out_shape,
        scratch_shapes=[pltpu.VMEM(rows3, F32), pltpu.VMEM(rows3, F32), pltpu.VMEM(rows3, F32),
                        pltpu.VMEM(rows3, BF16), pltpu.VMEM(rows3, BF16),
                        pltpu.VMEM((RET_HEADS, n_rows // CHUNK, 2 * CHUNK, HEAD_DIM), F32),
                        pltpu.VMEM((RET_HEADS * 4, CHUNK, CHUNK), F32)],
        compiler_params=_params(("arbitrary", "arbitrary")),
        name="retention_mixer",
    )(*args)
    return (res[0], res[1]) if with_ctx else (res[0], None)


def _na_row_start(r):
    return min(max(r - NA_WIN_R // 2, 0), GRID_H - NA_WIN_R)


def _na_key_start(blk):
    return min(max(blk * NA_BLK_ROWS - NA_WIN_R // 2, 0), GRID_H - NA_BLK_KROWS)


def _na_table_kernel(tab_ref, bias_ref, mask_ref):
    q = lax.broadcasted_iota(jnp.int32, (GRID_W, HEAD_DIM), 0)
    lane = lax.broadcasted_iota(jnp.int32, (GRID_W, HEAD_DIM), 1)
    kcol = lane % GRID_W
    lower_half = lane < GRID_W
    off_c = jnp.clip(kcol - q, -(NA_WIN_C - 1), NA_WIN_C - 1) + (NA_WIN_C - 1)
    c_start = jnp.clip(q - NA_WIN_C // 2, 0, GRID_W - NA_WIN_C)
    col_win = jnp.where((kcol >= c_start) & (kcol < c_start + NA_WIN_C), 1.0, 0.0)
    zeros = jnp.zeros((GRID_W, HEAD_DIM), F32)
    planes = []
    for o in range(2 * NA_WIN_R - 1):
        g = zeros
        for c in range(2 * NA_WIN_C - 1):
            g = jnp.where(off_c == c, tab_ref[0, o, c] * LOG2E, g)
        planes.append(g)
    n_blk = GRID_H // NA_BLK_ROWS
    for pat, blk in enumerate((0, 1, n_blk - 1)):
        k0 = _na_key_start(blk)
        for rq in range(NA_BLK_ROWS):
            r = blk * NA_BLK_ROWS + rq
            rs = _na_row_start(r)
            for jp in range(NA_BLK_KROWS // 2):
                b_half, m_half = [], []
                for key_row in (k0 + 2 * jp, k0 + 2 * jp + 1):
                    valid = rs <= key_row < rs + NA_WIN_R
                    b_half.append(planes[key_row - r + NA_WIN_R - 1] if valid else zeros)
                    m_half.append(col_win if valid else zeros)
                rows = slice(rq * GRID_W, (rq + 1) * GRID_W)
                lanes = slice(jp * HEAD_DIM, (jp + 1) * HEAD_DIM)
                bias_ref[0, pat, rows, lanes] = jnp.where(lower_half, b_half[0], b_half[1])
                mask_ref[0, pat, rows, lanes] = jnp.where(lower_half, m_half[0], m_half[1])


def _na_tables(na_bias_all):
    n = na_bias_all.shape[0]
    spec = pl.BlockSpec((1, NA_PATTERNS, NA_BLK_Q, NA_BLK_KEYS), lambda h: (h, 0, 0, 0))
    shape = jax.ShapeDtypeStruct((n, NA_PATTERNS, NA_BLK_Q, NA_BLK_KEYS), F32)
    return pl.pallas_call(
        _na_table_kernel,
        grid=(n,),
        in_specs=[pl.BlockSpec((1, 2 * NA_WIN_R - 1, 2 * NA_WIN_C - 1), lambda h: (h, 0, 0),
                               memory_space=pltpu.SMEM)],
        out_specs=[spec, spec],
        out_shape=[shape, shape],
        compiler_params=_params(("arbitrary",)),
        name="na_tables",
    )(na_bias_all)


def _na_kernel(*refs, with_ctx):
    if with_ctx:
        (q_ref, k_ref, v_ref, g_ref, qc_ref, kc_ref, vc_ref, gc_ref, bias_ref, mask_ref,
         o_ref, oc_ref, q_sc, k_sc, v_sc, kc_sc, vc_sc, scx_sc) = refs
    else:
        (q_ref, k_ref, v_ref, g_ref, kc_ref, vc_ref, bias_ref, mask_ref,
         o_ref, q_sc, k_sc, v_sc, kc_sc, vc_sc, scx_sc) = refs
    scale = ATTN_SCALE * LOG2E
    for hh in range(NA_HEADS):
        lanes = slice(hh * HEAD_DIM, (hh + 1) * HEAD_DIM)
        q_sc[hh] = (q_ref[:, lanes] * scale).astype(BF16)
        k_sc[hh] = k_ref[:, lanes].astype(BF16)
        kc_sc[hh] = kc_ref[:, lanes].astype(BF16)
        v_sc[hh, :, 0:HEAD_DIM] = v_ref[:, lanes].astype(BF16)
        v_sc[hh, :, HEAD_DIM:] = jnp.ones((SEQ, HEAD_DIM), BF16)
        vc_sc[hh, :, 0:HEAD_DIM] = vc_ref[:, lanes].astype(BF16)
        vc_sc[hh, :, HEAD_DIM:] = jnp.ones((CTX_LEN, HEAD_DIM), BF16)
        scx_sc[hh] = _dot_nt(q_sc[hh], kc_sc[hh])
    n_blk = GRID_H // NA_BLK_ROWS

    def blk_step(blk, carry):
        k0 = jnp.clip(blk * NA_BLK_ROWS - NA_WIN_R // 2, 0, GRID_H - NA_BLK_KROWS)
        pat = jnp.where(blk == 0, 0, jnp.where(blk == n_blk - 1, 2, 1))
        rows = pl.ds(pl.multiple_of(blk * NA_BLK_Q, NA_BLK_Q), NA_BLK_Q)
        keys = pl.ds(pl.multiple_of(k0 * GRID_W, GRID_W), NA_BLK_KEYS)
        in_win = mask_ref[0, pat] > 0.5
        for hh in range(NA_HEADS):
            lanes = slice(hh * HEAD_DIM, (hh + 1) * HEAD_DIM)
            s_nb = _dot_nt(q_sc[hh, rows, :], k_sc[hh, keys, :]) + bias_ref[hh, pat]
            s_nb = jnp.where(in_win, s_nb, MASK_VALUE)
            s_cx = scx_sc[hh, rows, :]
            m = jnp.maximum(s_nb.max(axis=-1, keepdims=True), s_cx.max(axis=-1, keepdims=True))
            o2 = jnp.dot(jnp.exp2(s_nb - m).astype(BF16), v_sc[hh, keys, :],
                         preferred_element_type=F32)
            o2 = o2 + jnp.dot(jnp.exp2(s_cx - m).astype(BF16), vc_sc[hh],
                              preferred_element_type=F32)
            o = o2[:, 0:HEAD_DIM] / o2[:, HEAD_DIM:HEAD_DIM + 1]
            o_ref[rows, lanes] = (o * _silu(g_ref[rows, lanes])).astype(o_ref.dtype)
        return carry

    lax.fori_loop(0, n_blk, blk_step, 0, unroll=NA_UNROLL)

    if with_ctx:
        for hh in range(NA_HEADS):
            lanes = slice(hh * HEAD_DIM, (hh + 1) * HEAD_DIM)
            s = _dot_nt((qc_ref[:, lanes] * scale).astype(BF16), kc_sc[hh])
            o = _softmax_pv([s], [vc_sc[hh, :, 0:HEAD_DIM]])
            oc_ref[:, lanes] = (o * _silu(gc_ref[:, lanes])).astype(oc_ref.dtype)


def _na_mixer(p_lat, p_ctx, ctx_cols, na_tabs, l, with_ctx):
    kern = functools.partial(_na_kernel, with_ctx=with_ctx)
    width = NA_HEADS * HEAD_DIM
    tab0 = l * (N_HEADS // NA_HEADS)

    def col(c0, rows):
        return pl.BlockSpec((rows, width), lambda b, h: (b, c0 // NA_HEADS + h))

    in_specs = [col(COL_D_Q, SEQ), col(COL_D_K, SEQ), col(COL_D_V, SEQ), col(COL_D_GATE, SEQ)]
    args = [p_lat, p_lat, p_lat, p_lat]
    if with_ctx:
        in_specs += [col(COL_D_Q, CTX_LEN), col(COL_D_K, CTX_LEN), col(COL_D_V, CTX_LEN),
                     col(COL_D_GATE, CTX_LEN)]
        args += [p_ctx, p_ctx, p_ctx, p_ctx]
    else:
        in_specs += [col(ctx_cols["d_k"], CTX_LEN), col(ctx_cols["d_v"], CTX_LEN)]
        args += [p_ctx, p_ctx]
    tab_shape = (NA_PATTERNS, NA_BLK_Q, NA_BLK_KEYS)
    in_specs += [pl.BlockSpec((NA_HEADS,) + tab_shape, lambda b, h: (tab0 + h, 0, 0, 0)),
                 pl.BlockSpec((1,) + tab_shape, lambda b, h: (0, 0, 0, 0))]
    args += list(na_tabs)
    out_specs = [pl.BlockSpec((SEQ, width), lambda b, h: (b, h))]
    out_shape = [jax.ShapeDtypeStruct((BATCH * SEQ, GROUP_W), BF16)]
    if with_ctx:
        out_specs.append(pl.BlockSpec((CTX_LEN, width), lambda b, h: (b, h)))
        out_shape.append(jax.ShapeDtypeStruct((BATCH * CTX_LEN, GROUP_W), BF16))
    res = pl.pallas_call(
        kern,
        grid=(BATCH, N_HEADS // NA_HEADS),
        in_specs=in_specs,
        out_specs=out_specs,
        out_shape=out_shape,
        scratch_shapes=[pltpu.VMEM((NA_HEADS, SEQ, HEAD_DIM), BF16),
                        pltpu.VMEM((NA_HEADS, SEQ, HEAD_DIM), BF16),
                        pltpu.VMEM((NA_HEADS, SEQ, 2 * HEAD_DIM), BF16),
                        pltpu.VMEM((NA_HEADS, CTX_LEN, HEAD_DIM), BF16),
                        pltpu.VMEM((NA_HEADS, CTX_LEN, 2 * HEAD_DIM), BF16),
                        pltpu.VMEM((NA_HEADS, SEQ, CTX_LEN), F32)],
        compiler_params=_params(("arbitrary", "arbitrary")),
        name="na_mixer",
    )(*args)
    return (res[0], res[1]) if with_ctx else (res[0], None)


def _rope_tables():
    t = jnp.arange(SEQ)
    row = (t // GRID_W).astype(F32)
    col = (t % GRID_W).astype(F32)
    half = HEAD_DIM // 2
    inv_freq = ROPE_THETA ** (-jnp.arange(0, half, 2, dtype=F32) / half)
    ang_r = row[:, None] * inv_freq[None, :]
    ang_c = col[:, None] * inv_freq[None, :]
    ang = jnp.concatenate([ang_r, ang_r, ang_c, ang_c], axis=-1)
    cos, sin = jnp.cos(ang), jnp.sin(ang)
    quarter = (jnp.arange(HEAD_DIM) // (HEAD_DIM // 4)) % 2
    sin_a = jnp.where(quarter[None, :] == 0, -sin, 0.0)
    sin_b = jnp.where(quarter[None, :] == 1, sin, 0.0)
    return cos, sin_a, sin_b


def kernel(x, c, ctx, c_ctx, ada_w, ada_b, norm_g, w_in, conv_w, conv_b, conv_ln_g, conv_ln_b,
           gqa_qn_g, gqa_kn_g, ret_decay_fwd, ret_decay_bwd, ret_gn_g, ret_gn_b, na_bias,
           w_out, final_g):
    rope_tabs = _rope_tables()
    x_lat = x.reshape(BATCH * SEQ, D_MODEL)
    x_ctx = ctx.reshape(BATCH * CTX_LEN, D_MODEL)
    c8 = jnp.concatenate(
        [c, c_ctx[None, :], jnp.zeros((MOD_ROWS - BATCH - 1, D_MODEL), F32)], axis=0)

    def vec3(a):
        return a.reshape(DEPTH, 1, a.shape[-1])

    ada_b3, norm_g3 = vec3(ada_b), vec3(norm_g)
    conv_b3, ln_g3, ln_b3 = vec3(conv_b), vec3(conv_ln_g), vec3(conv_ln_b)
    qn3, kn3 = vec3(gqa_qn_g), vec3(gqa_kn_g)
    gn_g3, gn_b3 = vec3(ret_gn_g), vec3(ret_gn_b)
    dec_f = jnp.broadcast_to(ret_decay_fwd[:, :, None, None], (DEPTH, N_HEADS, 1, HEAD_DIM))
    dec_b = jnp.broadcast_to(ret_decay_bwd[:, :, None, None], (DEPTH, N_HEADS, 1, HEAD_DIM))

    na_tabs = _na_tables(na_bias.reshape((DEPTH * N_HEADS,) + na_bias.shape[2:]))

    for l in range(DEPTH):
        with_ctx = l < DEPTH - 1
        mod = _ada_table(c8, ada_w, ada_b3, l)
        h_lat = _norm_modulate(x_lat, norm_g3, l, mod, SEQ, None)
        h_ctx = _norm_modulate(x_ctx, norm_g3, l, mod, None, MOD_CTX_ROW)
        p_lat = _in_proj(h_lat, w_in, l)
        p_ctx = _in_proj(h_ctx, w_in, l, kv_only=not with_ctx)
        ctx_cols = CTX_COLS_FULL if with_ctx else CTX_COLS_KV_ONLY

        ya_l = _conv_mixer(p_lat, SEQ, conv_w, conv_b3, ln_g3, ln_b3, l)
        yb_l, yb_c = _gqa_mixer(p_lat, p_ctx, ctx_cols, rope_tabs, qn3, kn3, l, with_ctx)
        yc_l, yc_c = _retention_mixer(p_lat, p_ctx, ctx_cols, rope_tabs, dec_f, dec_b, gn_g3, gn_b3,
                                      l, with_ctx)
        yd_l, yd_c = _na_mixer(p_lat, p_ctx, ctx_cols, na_tabs, l, with_ctx)

        x_new = _out_proj((ya_l, yb_l, yc_l, yd_l), w_out, l, x_lat, mod, SEQ, None)
        if with_ctx:
            ya_c = _conv_mixer(p_ctx, CTX_LEN, conv_w, conv_b3, ln_g3, ln_b3, l)
            x_ctx = _out_proj((ya_c, yb_c, yc_c, yd_c), w_out, l, x_ctx, mod, None, MOD_CTX_ROW)
        x_lat = x_new

    out = _final_norm(x_lat, final_g.reshape(1, D_MODEL))
    return out.reshape(BATCH, SEQ, D_MODEL)
```
